```python
import jax, jax.numpy as jnp
from jax import lax
import numpy as np

D_MODEL = 2048
BATCH = 2
SEQ = 16384
DEPTH = 1
DEC_BATCH = 8
DEC_SEQ = 32
PAST_LEN = 2048

CHUNK = 64
HEAD_DIM = 128
D_MIX = D_MODEL
RET_WIDTH = D_MIX // 2
FOX_WIDTH = D_MIX - RET_WIDTH
H_RET = RET_WIDTH // HEAD_DIM
H_FOX = FOX_WIDTH // HEAD_DIM
ROPE_BASE = 10000.0
Q_BLOCK = 128
N_EXPERTS = 256
TOP_K = 8
N_GROUPS = 8
TOPK_GROUPS = 4
D_EXPERT = D_MODEL // 4
ROUTED_SCALE = 2.5
EXPERT_BLOCK = 128
LN_EPS = 1e-5
ALPHA = (2.0 * DEPTH) ** 0.25
BETA = (8.0 * DEPTH) ** -0.25
IN_COLS = 4 * RET_WIDTH + 3 * FOX_WIDTH + H_FOX

kernel_name = "hymba_retention_fox_moe_deepnorm_step"


def layer_norm(x, g, b):
    xf = x.astype(jnp.float32)
    mu = xf.mean(-1, keepdims=True)
    var = jnp.square(xf - mu).mean(-1, keepdims=True)
    return ((xf - mu) * lax.rsqrt(var + LN_EPS) * g + b).astype(x.dtype)


def rotary(x, pos):
    half = HEAD_DIM // 2
    inv_freq = ROPE_BASE ** (-jnp.arange(half, dtype=jnp.float32) / half)
    ang = pos.astype(jnp.float32)[:, None] * inv_freq[None, :]
    cos = jnp.cos(ang)[None, :, None, :]
    sin = jnp.sin(ang)[None, :, None, :]
    x1, x2 = x[..., :half], x[..., half:]
    return jnp.concatenate([x1 * cos - x2 * sin, x1 * sin + x2 * cos], axis=-1)


def ret_log_gamma():
    return jnp.log1p(-jnp.exp2(-5.0 - jnp.arange(H_RET, dtype=jnp.float32)))


def retention_chunk(S, q, k, v, log_g):
    C = q.shape[1]
    i = jnp.arange(C, dtype=jnp.float32)
    rel = i[:, None] - i[None, :]
    decay = jnp.where(rel >= 0, jnp.exp(log_g[:, None, None] * jnp.maximum(rel, 0.0)), 0.0)
    scores = jnp.einsum('bihd,bjhd->bhij', q, k) * decay[None]
    o = jnp.einsum('bhij,bjhd->bihd', scores, v)
    q_dec = jnp.exp(log_g[None, :] * (i[:, None] + 1.0))
    o = o + jnp.einsum('bihd,bhde->bihe', q * q_dec[None, :, :, None], S)
    k_dec = jnp.exp(log_g[None, :] * (C - 1.0 - i[:, None]))
    S_new = (jnp.exp(log_g * C)[None, :, None, None] * S
             + jnp.einsum('bjhd,bjhe->bhde', k * k_dec[None, :, :, None], v))
    return S_new, o


def fox_attend(q, cq, qpos, k, v, ck, kpos):
    s = jnp.einsum('bqhd,bkhd->bhqk', q, k).astype(jnp.float32) * (HEAD_DIM ** -0.5)
    s = s + jnp.transpose(cq, (0, 2, 1))[:, :, :, None] - jnp.transpose(ck, (0, 2, 1))[:, :, None, :]
    s = jnp.where(kpos[None, :] <= qpos[:, None], s, -jnp.inf)
    p = jax.nn.softmax(s, axis=-1)
    return jnp.einsum('bhqk,bkhd->bqhd', p.astype(v.dtype), v)


def project(x, pos, w_in, b_fgate):
    B, T, _ = x.shape
    p = jnp.einsum('btd,de->bte', x, w_in)
    offs = np.cumsum([RET_WIDTH] * 4 + [FOX_WIDTH] * 3)
    rq, rk, rv, rg, fq, fk, fv, ff = jnp.split(p, offs, axis=-1)
    rq = rotary(rq.reshape(B, T, H_RET, HEAD_DIM).astype(jnp.float32), pos)
    rk = rotary(rk.reshape(B, T, H_RET, HEAD_DIM).astype(jnp.float32), pos) * (HEAD_DIM ** -0.5)
    rv = rv.reshape(B, T, H_RET, HEAD_DIM).astype(jnp.float32)
    fq = fq.reshape(B, T, H_FOX, HEAD_DIM)
    fk = fk.reshape(B, T, H_FOX, HEAD_DIM)
    fv = fv.reshape(B, T, H_FOX, HEAD_DIM)
    logf = jax.nn.log_sigmoid((ff + b_fgate).astype(jnp.float32))
    return rq, rk, rv, rg, fq, fk, fv, logf


def finish(ret_o, rg, fox_o, ret_gn_g, w_out):
    B, T = ret_o.shape[:2]
    mu = ret_o.mean(-1, keepdims=True)
    var = jnp.square(ret_o - mu).mean(-1, keepdims=True)
    yn = ((ret_o - mu) * lax.rsqrt(var + LN_EPS)).reshape(B, T, RET_WIDTH) * ret_gn_g
    ret_y = (jax.nn.silu(rg.astype(jnp.float32)) * yn).astype(rg.dtype)
    cat = jnp.concatenate([ret_y, fox_o.reshape(B, T, FOX_WIDTH).astype(rg.dtype)], axis=-1)
    return jnp.einsum('bte,ed->btd', cat, w_out)


def mixer_prompt(x, w_in, b_fgate, ret_gn_g, w_out):
    B, T, _ = x.shape
    pos = jnp.arange(T)
    rq, rk, rv, rg, fq, fk, fv, logf = project(x, pos, w_in, b_fgate)
    log_g = ret_log_gamma()
    n_chunks = T // CHUNK
    to_chunks = lambda t: t.reshape(B, n_chunks, CHUNK, H_RET, HEAD_DIM).swapaxes(0, 1)
    S0 = jnp.zeros((B, H_RET, HEAD_DIM, HEAD_DIM), jnp.float32)
    S_fin, ret_o = lax.scan(lambda S, qkv: retention_chunk(S, qkv[0], qkv[1], qkv[2], log_g),
                            S0, (to_chunks(rq), to_chunks(rk), to_chunks(rv)))
    ret_o = ret_o.swapaxes(0, 1).reshape(B, T, H_RET, HEAD_DIM)
    c = jnp.cumsum(logf, axis=1)
    n_qb = T // Q_BLOCK
    qb = fq.reshape(B, n_qb, Q_BLOCK, H_FOX, HEAD_DIM).swapaxes(0, 1)
    cb = c.reshape(B, n_qb, Q_BLOCK, H_FOX).swapaxes(0, 1)
    pb = pos.reshape(n_qb, Q_BLOCK)
    fox_o = lax.map(lambda a: fox_attend(a[0], a[1], a[2], fk, fv, c, pos), (qb, cb, pb))
    fox_o = fox_o.swapaxes(0, 1).reshape(B, T, H_FOX, HEAD_DIM)
    y = finish(ret_o, rg, fox_o, ret_gn_g, w_out)
    return y, S_fin, fk, fv, logf


def mixer_sample(x, state_ret, cache_k, cache_v, cache_logf, w_in, b_fgate, ret_gn_g, w_out):
    B, T, _ = x.shape
    past = cache_k.shape[1]
    pos = past + jnp.arange(T)
    rq, rk, rv, rg, fq, fk, fv, logf = project(x, pos, w_in, b_fgate)
    S_new, ret_o = retention_chunk(state_ret.astype(jnp.float32), rq, rk, rv, ret_log_gamma())
    k_all = jnp.concatenate([cache_k, fk], axis=1)
    v_all = jnp.concatenate([cache_v, fv], axis=1)
    c = jnp.cumsum(jnp.concatenate([cache_logf.astype(jnp.float32), logf], axis=1), axis=1)
    fox_o = fox_attend(fq, c[:, past:], pos, k_all, v_all, c, jnp.arange(past + T))
    y = finish(ret_o, rg, fox_o, ret_gn_g, w_out)
    return y, S_new, fk, fv, logf


def moe(x, w_router, e_bias, w_e_gate, w_e_up, w_e_down, w_s_gate, w_s_up, w_s_down):
    n_tok, d = x.shape
    scores = jax.nn.sigmoid(jnp.einsum('nd,de->ne', x, w_router).astype(jnp.float32))
    sel = scores + e_bias.astype(jnp.float32)
    grp = sel.reshape(n_tok, N_GROUPS, N_EXPERTS // N_GROUPS)
    grp_score = lax.top_k(grp, 2)[0].sum(-1)
    _, top_g = lax.top_k(grp_score, TOPK_GROUPS)
    gmask = jnp.any(top_g[:, :, None] == jnp.arange(N_GROUPS)[None, None, :], axis=1)
    sel = jnp.where(jnp.repeat(gmask, N_EXPERTS // N_GROUPS, axis=1), sel, -jnp.inf)
    _, idx = lax.top_k(sel, TOP_K)
    gate = jnp.take_along_axis(scores, idx, axis=1)
    gate = gate / gate.sum(-1, keepdims=True) * ROUTED_SCALE
    n_assign = n_tok * TOP_K
    n_blocks = -(-n_assign // EXPERT_BLOCK) + N_EXPERTS
    n_rows = n_blocks * EXPERT_BLOCK
    flat_e = idx.reshape(n_assign)
    flat_tok = jnp.repeat(jnp.arange(n_tok, dtype=jnp.int32), TOP_K)
    flat_gate = gate.reshape(n_assign)
    order = jnp.argsort(flat_e, stable=True)
    e_sorted = flat_e[order]
    counts = jnp.bincount(flat_e, length=N_EXPERTS)
    start = jnp.cumsum(counts) - counts
    padded = (counts + EXPERT_BLOCK - 1) // EXPERT_BLOCK * EXPERT_BLOCK
    pend = jnp.cumsum(padded)
    pstart = pend - padded
    dest = pstart[e_sorted] + jnp.arange(n_assign) - start[e_sorted]
    row_tok = jnp.full((n_rows,), n_tok, jnp.int32).at[dest].set(flat_tok[order])
    row_gate = jnp.zeros((n_rows,), jnp.float32).at[dest].set(flat_gate[order])
    block_e = jnp.minimum(jnp.searchsorted(pend, jnp.arange(n_blocks) * EXPERT_BLOCK, side='right'),
                          N_EXPERTS - 1)
    x_pad = jnp.concatenate([x, jnp.zeros((1, d), x.dtype)], axis=0)

    def expert_block(args):
        rt, e, rgate = args
        xb = x_pad[rt]
        h = jax.nn.silu(xb @ w_e_gate[e]) * (xb @ w_e_up[e])
        return ((h @ w_e_down[e]) * rgate[:, None]).astype(x.dtype)

    out = lax.map(expert_block, (row_tok.reshape(n_blocks, EXPERT_BLOCK), block_e,
                                 row_gate.reshape(n_blocks, EXPERT_BLOCK)))
    routed = jax.ops.segment_sum(out.reshape(n_rows, d), row_tok, num_segments=n_tok + 1)[:n_tok]
    shared = (jax.nn.silu(x @ w_s_gate) * (x @ w_s_up)) @ w_s_down
    return routed + shared


def layer_tail(x, mix, ln1_g, ln1_b, w_router, e_bias, w_e_gate, w_e_up, w_e_down,
               w_s_gate, w_s_up, w_s_down, ln2_g, ln2_b):
    B, T, D = x.shape
    h = layer_norm(ALPHA * x + mix, ln1_g, ln1_b)
    f = moe(h.reshape(B * T, D), w_router, e_bias, w_e_gate, w_e_up, w_e_down,
            w_s_gate, w_s_up, w_s_down).reshape(B, T, D)
    return layer_norm(ALPHA * h + f, ln2_g, ln2_b)


def setup_inputs(seed: int = 0) -> dict:
    key = jax.random.key(seed)
    ks = jax.random.split(key, 24)
    nrm = lambda k, s: jax.random.normal(k, s, jnp.float32)
    col_scale = jnp.concatenate([
        jnp.ones((2 * RET_WIDTH,)), jnp.full((RET_WIDTH,), BETA), jnp.ones((RET_WIDTH,)),
        jnp.ones((2 * FOX_WIDTH,)), jnp.full((FOX_WIDTH,), BETA), jnp.ones((H_FOX,))]).astype(jnp.float32)
    return {
        "x_prompt": nrm(ks[0], (BATCH, SEQ, D_MODEL)),
        "x_sample": nrm(ks[1], (DEC_BATCH, DEC_SEQ, D_MODEL)),
        "state_ret": 0.5 * nrm(ks[2], (DEPTH, DEC_BATCH, H_RET, HEAD_DIM, HEAD_DIM)),
        "cache_fox_k": nrm(ks[3], (DEPTH, DEC_BATCH, PAST_LEN, H_FOX, HEAD_DIM)),
        "cache_fox_v": BETA * nrm(ks[4], (DEPTH, DEC_BATCH, PAST_LEN, H_FOX, HEAD_DIM)),
        "cache_fox_logf": jax.nn.log_sigmoid(3.0 + nrm(ks[5], (DEPTH, DEC_BATCH, PAST_LEN, H_FOX))),
        "w_in": nrm(ks[6], (DEPTH, D_MODEL, IN_COLS)) * (D_MODEL ** -0.5) * col_scale,
        "b_fgate": jnp.linspace(1.0, 5.0, H_FOX, dtype=jnp.float32) + 0.1 * nrm(ks[7], (DEPTH, H_FOX)),
        "ret_gn_g": 1.0 + 0.1 * nrm(ks[8], (DEPTH, RET_WIDTH)),
        "w_out": nrm(ks[9], (DEPTH, D_MIX, D_MODEL)) * (D_MIX ** -0.5) * BETA,
        "ln1_g": 1.0 + 0.1 * nrm(ks[10], (DEPTH, D_MODEL)),
        "ln1_b": 0.02 * nrm(ks[11], (DEPTH, D_MODEL)),
        "w_router": nrm(ks[12], (DEPTH, D_MODEL, N_EXPERTS)) * (D_MODEL ** -0.5),
        "e_bias": 0.01 * nrm(ks[13], (DEPTH, N_EXPERTS)),
        "w_e_gate": nrm(ks[14], (DEPTH, N_EXPERTS, D_MODEL, D_EXPERT)) * (D_MODEL ** -0.5) * BETA,
        "w_e_up": nrm(ks[15], (DEPTH, N_EXPERTS, D_MODEL, D_EXPERT)) * (D_MODEL ** -0.5) * BETA,
        "w_e_down": nrm(ks[16], (DEPTH, N_EXPERTS, D_EXPERT, D_MODEL)) * (D_EXPERT ** -0.5) * BETA,
        "w_s_gate": nrm(ks[17], (DEPTH, D_MODEL, D_EXPERT)) * (D_MODEL ** -0.5) * BETA,
        "w_s_up": nrm(ks[18], (DEPTH, D_MODEL, D_EXPERT)) * (D_MODEL ** -0.5) * BETA,
        "w_s_down": nrm(ks[19], (DEPTH, D_EXPERT, D_MODEL)) * (D_EXPERT ** -0.5) * BETA,
        "ln2_g": 1.0 + 0.1 * nrm(ks[20], (DEPTH, D_MODEL)),
        "ln2_b": 0.02 * nrm(ks[21], (DEPTH, D_MODEL)),
    }


def reference(x_prompt, x_sample, state_ret, cache_fox_k, cache_fox_v, cache_fox_logf,
              w_in, b_fgate, ret_gn_g, w_out, ln1_g, ln1_b, w_router, e_bias,
              w_e_gate, w_e_up, w_e_down, w_s_gate, w_s_up, w_s_down, ln2_g, ln2_b):
    xp, xs = x_prompt, x_sample
    rs_p, k_p, v_p, lf_p = [], [], [], []
    rs_s, k_s, v_s, lf_s = [], [], [], []
    for l in range(DEPTH):
        mp, Sp, kp, vp, lfp = mixer_prompt(xp, w_in[l], b_fgate[l], ret_gn_g[l], w_out[l])
        ms, Ss, kn, vn, lfn = mixer_sample(xs, state_ret[l], cache_fox_k[l], cache_fox_v[l],
                                           cache_fox_logf[l], w_in[l], b_fgate[l], ret_gn_g[l], w_out[l])
        rs_p.append(Sp.astype(x_prompt.dtype)); k_p.append(kp); v_p.append(vp)
        lf_p.append(lfp.astype(x_prompt.dtype))
        rs_s.append(Ss.astype(state_ret.dtype)); k_s.append(kn); v_s.append(vn)
        lf_s.append(lfn.astype(cache_fox_logf.dtype))
        xp = layer_tail(xp, mp, ln1_g[l], ln1_b[l], w_router[l], e_bias[l], w_e_gate[l], w_e_up[l],
                        w_e_down[l], w_s_gate[l], w_s_up[l], w_s_down[l], ln2_g[l], ln2_b[l])
        xs = layer_tail(xs, ms, ln1_g[l], ln1_b[l], w_router[l], e_bias[l], w_e_gate[l], w_e_up[l],
                        w_e_down[l], w_s_gate[l], w_s_up[l], w_s_down[l], ln2_g[l], ln2_b[l])
    return (xp, xs, jnp.stack(rs_p), jnp.stack(k_p), jnp.stack(v_p), jnp.stack(lf_p),
            jnp.stack(rs_s), jnp.stack(k_s), jnp.stack(v_s), jnp.stack(lf_s))
```

```python
import functools
import math

import jax
import jax.numpy as jnp
from jax import lax
from jax.experimental import pallas as pl
from jax.experimental.pallas import tpu as pltpu

HEAD_DIM = 128
ROPE_BASE = 10000.0
N_GROUPS = 8
TOPK_GROUPS = 4
TOP_K = 8
ROUTED_SCALE = 2.5
LN_EPS = 1e-5
LANES = 128
EXPERT_ROWS = 256
VMEM_LIMIT_BYTES = 56 * 1024 * 1024

F32 = jnp.float32
BF16 = jnp.bfloat16
U32 = jnp.uint32
I32 = jnp.int32


def _cparams(sem):
    return pltpu.CompilerParams(dimension_semantics=sem, vmem_limit_bytes=VMEM_LIMIT_BYTES)


def _tile(n, pref):
    t = min(n, pref)
    while n % t:
        t -= 1
    return t


def _dot(a, b):
    return jnp.dot(a, b, preferred_element_type=F32)


def _dot_nt(a, b):
    return lax.dot_general(a, b, (((1,), (1,)), ((), ())), preferred_element_type=F32)


def _dot_tn(a, b):
    return lax.dot_general(a, b, (((0,), (0,)), ((), ())), preferred_element_type=F32)


def _pack_pairs(a):
    n = a.shape[1] // 2
    lo = lax.bitcast_convert_type(a[:, :n].astype(BF16).astype(F32), U32)
    hi = lax.bitcast_convert_type(a[:, n:].astype(BF16).astype(F32), U32)
    return (lo >> 16) | (hi & jnp.uint32(0xFFFF0000))


def _unpack_pairs(u):
    lo = lax.bitcast_convert_type(u << 16, F32)
    hi = lax.bitcast_convert_type(u & jnp.uint32(0xFFFF0000), F32)
    return lo, hi


def _layer_norm(z, g, b):
    mu = jnp.mean(z, axis=-1, keepdims=True)
    zc = z - mu
    var = jnp.mean(zc * zc, axis=-1, keepdims=True)
    return zc * lax.rsqrt(var + LN_EPS) * g + b


def _silu(g):
    return g * jax.nn.sigmoid(g)


def _proj_ret_kernel(x_ref, w_ref, cs_ref, sn_ref, o_ref, *, width, scale):
    xb = x_ref[...].astype(BF16)
    cs = cs_ref[...]
    sn = sn_ref[...]
    for sec in range(4):
        p = _dot(xb, w_ref[:, sec * width:(sec + 1) * width])
        if sec < 2:
            for h in range(width // HEAD_DIM):
                ph = p[:, h * HEAD_DIM:(h + 1) * HEAD_DIM]
                r = ph * cs + pltpu.roll(ph, HEAD_DIM // 2, 1) * sn
                if sec == 1:
                    r = r * scale
                o_ref[:, sec * width + h * HEAD_DIM:sec * width + (h + 1) * HEAD_DIM] = r.astype(BF16)
        else:
            o_ref[:, sec * width:(sec + 1) * width] = p.astype(BF16)


def _proj_ret(x, w, cs, sn, seq, tm):
    n, d = x.shape
    width = w.shape[1] // 4
    nt_seq = seq // tm
    return pl.pallas_call(
        functools.partial(_proj_ret_kernel, width=width, scale=HEAD_DIM ** -0.5),
        grid=(n // tm,),
        in_specs=[
            pl.BlockSpec((tm, d), lambda i: (i, 0)),
            pl.BlockSpec((d, 4 * width), lambda i: (0, 0)),
            pl.BlockSpec((tm, HEAD_DIM), lambda i: (i % nt_seq, 0)),
            pl.BlockSpec((tm, HEAD_DIM), lambda i: (i % nt_seq, 0)),
        ],
        out_specs=pl.BlockSpec((tm, 4 * width), lambda i: (i, 0)),
        out_shape=jax.ShapeDtypeStruct((n, 4 * width), BF16),
        compiler_params=_cparams(("arbitrary",)),
    )(x, w, cs, sn)


def _proj_fox_kernel(x_ref, w_ref, wf_ref, bf_ref, q_ref, k_ref, v_ref, kb_ref, vb_ref, lf_ref,
                     *, width, scale):
    xb = x_ref[...].astype(BF16)
    q = _dot(xb, w_ref[:, :width])
    q_ref[...] = (q * scale).astype(BF16)
    k = _dot(xb, w_ref[:, width:2 * width])
    k_ref[...] = k
    kb_ref[...] = k.astype(BF16)
    v = _dot(xb, w_ref[:, 2 * width:])
    v_ref[...] = v
    vb_ref[...] = v.astype(BF16)
    z = _dot(xb, wf_ref[...]) + bf_ref[...]
    lf_ref[...] = jnp.minimum(z, 0.0) - jnp.log1p(jnp.exp(-jnp.abs(z)))


def _proj_fox(x, w, wf, bfg, tm):
    n, d = x.shape
    width = w.shape[1] // 3
    row = lambda c: pl.BlockSpec((tm, c), lambda i: (i, 0))
    return pl.pallas_call(
        functools.partial(_proj_fox_kernel, width=width, scale=HEAD_DIM ** -0.5),
        grid=(n // tm,),
        in_specs=[
            row(d),
            pl.BlockSpec((d, 3 * width), lambda i: (0, 0)),
            pl.BlockSpec((d, LANES), lambda i: (0, 0)),
            pl.BlockSpec((1, LANES), lambda i: (0, 0)),
        ],
        out_specs=[row(width), row(width), row(width), row(width), row(width), row(LANES)],
        out_shape=[
            jax.ShapeDtypeStruct((n, width), BF16),
            jax.ShapeDtypeStruct((n, width), F32),
            jax.ShapeDtypeStruct((n, width), F32),
            jax.ShapeDtypeStruct((n, width), BF16),
            jax.ShapeDtypeStruct((n, width), BF16),
            jax.ShapeDtypeStruct((n, LANES), F32),
        ],
        compiler_params=_cparams(("arbitrary",)),
    )(x, w, wf, bfg)


def _cumsum_kernel(x_ref, o_ref, carry, *, tm):
    @pl.when(pl.program_id(1) == 0)
    def _():
        carry[...] = jnp.zeros_like(carry)

    x = x_ref[0]
    r = lax.broadcasted_iota(I32, (tm, tm), 0)
    c = lax.broadcasted_iota(I32, (tm, tm), 1)
    tri = jnp.where(c <= r, 1.0, 0.0).astype(BF16)
    hi = x.astype(BF16)
    r1 = x - hi.astype(F32)
    mid = r1.astype(BF16)
    lo = (r1 - mid.astype(F32)).astype(BF16)
    out = _dot(tri, hi) + _dot(tri, mid) + _dot(tri, lo) + carry[...]
    o_ref[0] = out
    carry[...] = out[tm - 1:tm, :]


def _cumsum(x, tm):
    b, t, _ = x.shape
    return pl.pallas_call(
        functools.partial(_cumsum_kernel, tm=tm),
        grid=(b, t // tm),
        in_specs=[pl.BlockSpec((1, tm, LANES), lambda i, j: (i, j, 0))],
        out_specs=pl.BlockSpec((1, tm, LANES), lambda i, j: (i, j, 0)),
        out_shape=jax.ShapeDtypeStruct(x.shape, F32),
        scratch_shapes=[pltpu.VMEM((1, LANES), F32)],
        compiler_params=_cparams(("arbitrary", "arbitrary")),
    )(x)


def _retention_kernel(q_ref, k_ref, v_ref, g_ref, s0_ref, gn_ref, y_ref, sout_ref, s_scr,
                      *, n_heads, chunk):
    c = pl.program_id(1)

    @pl.when(c == 0)
    def _():
        s_scr[...] = s0_ref[0]

    row = lax.broadcasted_iota(I32, (chunk, chunk), 0)
    col = lax.broadcasted_iota(I32, (chunk, chunk), 1)
    rel = (row - col).astype(F32)
    ri = lax.broadcasted_iota(I32, (chunk, HEAD_DIM), 0).astype(F32)
    for h in range(n_heads):
        sl = slice(h * HEAD_DIM, (h + 1) * HEAD_DIM)
        lg = math.log1p(-(2.0 ** (-5 - h)))
        decay = jnp.where(rel >= 0, jnp.exp(lg * jnp.maximum(rel, 0.0)), 0.0)
        q = q_ref[0, :, sl]
        k = k_ref[0, :, sl]
        v = v_ref[0, :, sl]
        state = s_scr[h]
        scores = _dot_nt(q, k) * decay
        o = _dot(scores.astype(BF16), v)
        o = o + jnp.exp(lg * (ri + 1.0)) * _dot(q, state.astype(BF16))
        kd = (k.astype(F32) * jnp.exp(lg * (chunk - 1.0 - ri))).astype(BF16)
        s_scr[h] = math.exp(lg * chunk) * state + _dot_tn(kd, v)
        mu = jnp.mean(o, axis=-1, keepdims=True)
        oc = o - mu
        var = jnp.mean(oc * oc, axis=-1, keepdims=True)
        yn = oc * lax.rsqrt(var + LN_EPS) * gn_ref[:, sl]
        y_ref[0, :, sl] = (_silu(g_ref[0, :, sl].astype(F32)) * yn).astype(BF16)

    @pl.when(c == pl.num_programs(1) - 1)
    def _():
        sout_ref[0] = s_scr[...]


def _retention(p, s0, gn, chunk):
    b, t, w4 = p.shape
    width = w4 // 4
    n_heads = width // HEAD_DIM
    sec = lambda s: pl.BlockSpec((1, chunk, width), lambda i, j: (i, j, s))
    st = pl.BlockSpec((1, n_heads, HEAD_DIM, HEAD_DIM), lambda i, j: (i, 0, 0, 0))
    return pl.pallas_call(
        functools.partial(_retention_kernel, n_heads=n_heads, chunk=chunk),
        grid=(b, t // chunk),
        in_specs=[sec(0), sec(1), sec(2), sec(3), st, pl.BlockSpec((1, width), lambda i, j: (0, 0))],
        out_specs=[pl.BlockSpec((1, chunk, width), lambda i, j: (i, j, 0)), st],
        out_shape=[jax.ShapeDtypeStruct((b, t, width), BF16),
                   jax.ShapeDtypeStruct(s0.shape, F32)],
        scratch_shapes=[pltpu.VMEM((n_heads, HEAD_DIM, HEAD_DIM), F32)],
        compiler_params=_cparams(("arbitrary", "arbitrary")),
    )(p, p, p, p, s0, gn)


def _fox_kernel(q_ref, k_ref, v_ref, cq_ref, ck_ref, o_ref, *, tq, tk, q_off, n_kblocks):
    h = pl.program_id(1)
    qi = pl.program_id(2)
    q = q_ref[0]
    lane = lax.broadcasted_iota(I32, (tq, LANES), 1)
    cq = jnp.sum(jnp.where(lane == h, cq_ref[0], 0.0), axis=-1, keepdims=True)
    q_first = q_off + qi * tq
    n_full = jnp.minimum((q_first + 1) // tk, n_kblocks)
    n_tot = jnp.minimum((q_first + tq + tk - 1) // tk, n_kblocks)

    def step(j, carry, masked):
        m, l, acc = carry
        k0 = pl.multiple_of(j * tk, tk)
        k = k_ref[0, pl.ds(k0, tk), :]
        v = v_ref[0, pl.ds(k0, tk), :]
        ck = ck_ref[0, 0, :, pl.ds(k0, tk)]
        s = _dot_nt(q, k) + cq - ck
        if masked:
            qpos = q_first + lax.broadcasted_iota(I32, (tq, tk), 0)
            kpos = k0 + lax.broadcasted_iota(I32, (tq, tk), 1)
            s = jnp.where(kpos <= qpos, s, -jnp.inf)
        m_new = jnp.maximum(m, jnp.max(s, axis=-1, keepdims=True))
        alpha = jnp.exp(m - m_new)
        p = jnp.exp(s - m_new)
        l = alpha * l + jnp.sum(p, axis=-1, keepdims=True)
        acc = alpha * acc + _dot(p.astype(BF16), v)
        return m_new, l, acc

    carry = (jnp.full((tq, 1), -jnp.inf, F32), jnp.zeros((tq, 1), F32), jnp.zeros((tq, HEAD_DIM), F32))
    carry = lax.fori_loop(0, n_full, functools.partial(step, masked=False), carry)
    _, l, acc = lax.fori_loop(n_full, n_tot, functools.partial(step, masked=True), carry)
    o_ref[0] = (acc / l).astype(BF16)


def _fox(q, k, v, c, c_row, q_off, tq, tk):
    b, t_q, width = q.shape
    t_k = k.shape[1]
    n_heads = width // HEAD_DIM
    qb0 = q_off // tq
    return pl.pallas_call(
        functools.partial(_fox_kernel, tq=tq, tk=tk, q_off=q_off, n_kblocks=t_k // tk),
        grid=(b, n_heads, t_q // tq),
        in_specs=[
            pl.BlockSpec((1, tq, HEAD_DIM), lambda i, h, j: (i, j, h)),
            pl.BlockSpec((1, t_k, HEAD_DIM), lambda i, h, j: (i, 0, h)),
            pl.BlockSpec((1, t_k, HEAD_DIM), lambda i, h, j: (i, 0, h)),
            pl.BlockSpec((1, tq, LANES), lambda i, h, j: (i, j + qb0, 0)),
            pl.BlockSpec((1, 1, 1, t_k), lambda i, h, j: (i, h, 0, 0)),
        ],
        out_specs=pl.BlockSpec((1, tq, HEAD_DIM), lambda i, h, j: (i, j, h)),
        out_shape=jax.ShapeDtypeStruct((b, t_q, width), BF16),
        compiler_params=_cparams(("arbitrary", "arbitrary", "arbitrary")),
    )(q, k, v, c, c_row)


def _finish_kernel(x_ref, ry_ref, fo_ref, w_ref, g_ref, b_ref, h_ref, hp_ref, *, alpha, half):
    mix = _dot(ry_ref[...], w_ref[:half, :]) + _dot(fo_ref[...], w_ref[half:, :])
    h = _layer_norm(alpha * x_ref[...] + mix, g_ref[...], b_ref[...])
    h_ref[...] = h
    hp_ref[...] = _pack_pairs(h)


def _finish(x, ry, fo, w, g, b, alpha, tm):
    n, d = x.shape
    half = ry.shape[1]
    row = lambda c: pl.BlockSpec((tm, c), lambda i: (i, 0))
    vec = pl.BlockSpec((1, d), lambda i: (0, 0))
    return pl.pallas_call(
        functools.partial(_finish_kernel, alpha=alpha, half=half),
        grid=(n // tm,),
        in_specs=[row(d), row(half), row(fo.shape[1]), pl.BlockSpec(w.shape, lambda i: (0, 0)), vec, vec],
        out_specs=[row(d), row(d // 2)],
        out_shape=[jax.ShapeDtypeStruct((n, d), F32), jax.ShapeDtypeStruct((n, d // 2), U32)],
        compiler_params=_cparams(("arbitrary",)),
    )(x, ry, fo, w, g, b)


def _router_kernel(h_ref, whi_ref, wlo_ref, eb_ref, c0_ref, idx_ref, gate_ref, rank_ref, cnt_ref, carry,
                   *, n_exp, tm):
    @pl.when(pl.program_id(0) == 0)
    def _():
        carry[...] = c0_ref[...].astype(F32)

    h = h_ref[...]
    hhi = h.astype(BF16)
    hlo = (h - hhi.astype(F32)).astype(BF16)
    whi = whi_ref[...]
    logits = _dot_nt(whi, hhi) + _dot_nt(whi, hlo) + _dot_nt(wlo_ref[...], hhi)
    scores = jax.nn.sigmoid(logits)
    sel = scores + eb_ref[...]
    gsz = n_exp // N_GROUPS
    eio = lax.broadcasted_iota(I32, (n_exp, tm), 0).astype(F32)
    gio = lax.broadcasted_iota(I32, (gsz, tm), 0).astype(F32)
    gs_rows = []
    for g in range(N_GROUPS):
        sg = sel[g * gsz:(g + 1) * gsz]
        m1 = jnp.max(sg, axis=0, keepdims=True)
        i1 = jnp.min(jnp.where(sg == m1, gio, float(gsz)), axis=0, keepdims=True)
        m2 = jnp.max(jnp.where(gio == i1, -jnp.inf, sg), axis=0, keepdims=True)
        gs_rows.append(m1 + m2)
    gs = jnp.concatenate(gs_rows, axis=0)
    grow = lax.broadcasted_iota(I32, (N_GROUPS, tm), 0)
    beaten = jnp.zeros((N_GROUPS, tm), F32)
    for g2 in range(N_GROUPS):
        o = gs_rows[g2]
        beats = jnp.where(o > gs, 1.0, jnp.where((o == gs) & (grow > g2), 1.0, 0.0))
        beaten = beaten + beats
    gkeep = jnp.where(beaten < float(TOPK_GROUPS), 1.0, 0.0)
    selm = jnp.concatenate(
        [jnp.where(gkeep[g:g + 1] > 0.5, sel[g * gsz:(g + 1) * gsz], -jnp.inf) for g in range(N_GROUPS)],
        axis=0)
    member = jnp.zeros((n_exp, tm), F32)
    idxs, gates = [], []
    for _ in range(TOP_K):
        m = jnp.max(selm, axis=0, keepdims=True)
        ik = jnp.min(jnp.where(selm == m, eio, float(n_exp)), axis=0, keepdims=True)
        hit = eio == ik
        gates.append(jnp.sum(jnp.where(hit, scores, 0.0), axis=0, keepdims=True))
        idxs.append(ik)
        selm = jnp.where(hit, -jnp.inf, selm)
        member = jnp.where(hit, 1.0, member)
    gsum = gates[0]
    for gk in gates[1:]:
        gsum = gsum + gk
    tr = lax.broadcasted_iota(I32, (tm, tm), 0)
    tc = lax.broadcasted_iota(I32, (tm, tm), 1)
    before = jnp.where(tr < tc, 1.0, 0.0).astype(BF16)
    prefix = _dot(member.astype(BF16), before) + carry[:, :1]
    ranks = [jnp.sum(jnp.where(eio == ik, prefix, 0.0), axis=0, keepdims=True) for ik in idxs]
    carry[...] = carry[...] + jnp.sum(member, axis=1, keepdims=True)
    idx_ref[...] = jnp.concatenate(idxs, axis=0).astype(I32)
    gate_ref[...] = jnp.concatenate([gk / gsum * ROUTED_SCALE for gk in gates], axis=0)
    rank_ref[...] = jnp.concatenate(ranks, axis=0).astype(I32)
    cnt_ref[...] = carry[...].astype(I32)


def _router(h, whi, wlo, eb, cnt0, tm):
    n, d = h.shape
    n_exp = whi.shape[0]
    tok = pl.BlockSpec((TOP_K, tm), lambda i: (0, i))
    full = lambda s: pl.BlockSpec(s, lambda i: (0, 0))
    return pl.pallas_call(
        functools.partial(_router_kernel, n_exp=n_exp, tm=tm),
        grid=(n // tm,),
        in_specs=[pl.BlockSpec((tm, d), lambda i: (i, 0)), full((n_exp, d)), full((n_exp, d)),
                  full((n_exp, 1)), full((n_exp, LANES))],
        out_specs=[tok, tok, tok, full((n_exp, LANES))],
        out_shape=[jax.ShapeDtypeStruct((TOP_K, n), I32), jax.ShapeDtypeStruct((TOP_K, n), F32),
                   jax.ShapeDtypeStruct((TOP_K, n), I32), jax.ShapeDtypeStruct((n_exp, LANES), I32)],
        scratch_shapes=[pltpu.VMEM((n_exp, LANES), F32)],
        compiler_params=_cparams(("arbitrary",)),
    )(h, whi, wlo, eb, cnt0)


def _dispatch_kernel(dest_ref, hp_ref, xs_in_ref, xs_ref, dsm, sem_d, sem, *, tm):
    del xs_in_ref
    i = pl.program_id(0)
    cp = pltpu.make_async_copy(dest_ref.at[i], dsm, sem_d)
    cp.start()
    cp.wait()

    def row_copy(t, d):
        return pltpu.make_async_copy(hp_ref.at[pl.ds(t, 1)], xs_ref.at[pl.ds(d, 1)], sem)

    def issue(t, carry):
        for k in range(TOP_K):
            row_copy(t, dsm[k, t]).start()
        return carry

    lax.fori_loop(0, tm, issue, 0)

    def drain(t, carry):
        for k in range(TOP_K):
            row_copy(0, 0).wait()
        return carry

    lax.fori_loop(0, tm, drain, 0)


def _dispatch(dest_tiles, hp, xs, tm):
    n, half = hp.shape
    return pl.pallas_call(
        functools.partial(_dispatch_kernel, tm=tm),
        grid=(n // tm,),
        in_specs=[pl.BlockSpec(memory_space=pl.ANY),
                  pl.BlockSpec((tm, half), lambda i: (i, 0)),
                  pl.BlockSpec(memory_space=pl.ANY)],
        out_specs=pl.BlockSpec(memory_space=pl.ANY),
        out_shape=jax.ShapeDtypeStruct(xs.shape, U32),
        scratch_shapes=[pltpu.SMEM((TOP_K, tm), I32), pltpu.SemaphoreType.DMA(()), pltpu.SemaphoreType.DMA(())],
        input_output_aliases={2: 0},
        compiler_params=_cparams(("arbitrary",)),
    )(dest_tiles, hp, xs)


def _experts_kernel(be_ref, nu_ref, x_ref, wg_ref, wu_ref, wd_ref, y_ref, *, half):
    del be_ref

    @pl.when(pl.program_id(0) < nu_ref[0])
    def _():
        lo, hi = _unpack_pairs(x_ref[...])
        lo = lo.astype(BF16)
        hi = hi.astype(BF16)
        g = _dot(lo, wg_ref[0, :half, :]) + _dot(hi, wg_ref[0, half:, :])
        u = _dot(lo, wu_ref[0, :half, :]) + _dot(hi, wu_ref[0, half:, :])
        hm = (_silu(g) * u).astype(BF16)
        y_ref[...] = _pack_pairs(_dot(hm, wd_ref[0]))


def _experts(block_e, n_used, xs, wg, wu, wd):
    n_rows, half = xs.shape
    n_exp, d, de = wg.shape
    nb = n_rows // EXPERT_ROWS
    blk = lambda b, be, nu: (jnp.minimum(b, nu[0] - 1), 0)
    wsel = lambda b, be, nu: (be[jnp.minimum(b, nu[0] - 1)], 0, 0)
    grid_spec = pltpu.PrefetchScalarGridSpec(
        num_scalar_prefetch=2,
        grid=(nb,),
        in_specs=[pl.BlockSpec((EXPERT_ROWS, half), blk),
                  pl.BlockSpec((1, d, de), wsel),
                  pl.BlockSpec((1, d, de), wsel),
                  pl.BlockSpec((1, de, d), wsel)],
        out_specs=pl.BlockSpec((EXPERT_ROWS, half), blk),
    )
    return pl.pallas_call(
        functools.partial(_experts_kernel, half=half),
        grid_spec=grid_spec,
        out_shape=jax.ShapeDtypeStruct((n_rows, half), U32),
        compiler_params=_cparams(("arbitrary",)),
    )(block_e, n_used, xs, wg, wu, wd)


def _combine_kernel(dest_ref, h_ref, gate_ref, ys_ref, wsg_ref, wsu_ref, wsd_ref, g_ref, b_ref, y_ref,
                    buf, dsm, sem_d, sem, *, tm, alpha):
    i = pl.program_id(0)
    cp = pltpu.make_async_copy(dest_ref.at[i], dsm, sem_d)
    cp.start()
    cp.wait()

    def row_copy(k, t, d):
        return pltpu.make_async_copy(ys_ref.at[pl.ds(d, 1)], buf.at[k, pl.ds(t, 1)], sem)

    def issue(t, carry):
        for k in range(TOP_K):
            row_copy(k, t, dsm[k, t]).start()
        return carry

    lax.fori_loop(0, tm, issue, 0)

    h = h_ref[...]
    hb = h.astype(BF16)
    sh = _dot((_silu(_dot(hb, wsg_ref[...])) * _dot(hb, wsu_ref[...])).astype(BF16), wsd_ref[...])

    def drain(t, carry):
        for k in range(TOP_K):
            row_copy(0, 0, 0).wait()
        return carry

    lax.fori_loop(0, tm, drain, 0)

    gate = gate_ref[...]
    acc_lo = jnp.zeros((tm, buf.shape[2]), F32)
    acc_hi = jnp.zeros((tm, buf.shape[2]), F32)
    for k in range(TOP_K):
        lo, hi = _unpack_pairs(buf[k])
        gk = gate[:, k:k + 1]
        acc_lo = acc_lo + gk * lo
        acc_hi = acc_hi + gk * hi
    routed = jnp.concatenate([acc_lo, acc_hi], axis=1)
    y_ref[...] = _layer_norm(alpha * h + (routed + sh), g_ref[...], b_ref[...])


def _combine(dest_tiles, h, gate_t, ys, wsg, wsu, wsd, g, b, alpha, tm):
    n, d = h.shape
    half = ys.shape[1]
    full = lambda a: pl.BlockSpec(a.shape, lambda i: (0, 0))
    return pl.pallas_call(
        functools.partial(_combine_kernel, tm=tm, alpha=alpha),
        grid=(n // tm,),
        in_specs=[pl.BlockSpec(memory_space=pl.ANY),
                  pl.BlockSpec((tm, d), lambda i: (i, 0)),
                  pl.BlockSpec((tm, TOP_K), lambda i: (i, 0)),
                  pl.BlockSpec(memory_space=pl.ANY),
                  full(wsg), full(wsu), full(wsd), full(g), full(b)],
        out_specs=pl.BlockSpec((tm, d), lambda i: (i, 0)),
        out_shape=jax.ShapeDtypeStruct((n, d), F32),
        scratch_shapes=[pltpu.VMEM((TOP_K, tm, half), U32), pltpu.SMEM((TOP_K, tm), I32),
                        pltpu.SemaphoreType.DMA(()), pltpu.SemaphoreType.DMA(())],
        compiler_params=_cparams(("arbitrary",)),
    )(dest_tiles, h, gate_t, ys, wsg, wsu, wsd, g, b)


def _rope_tables(pos):
    half = HEAD_DIM // 2
    inv_freq = ROPE_BASE ** (-jnp.arange(half, dtype=F32) / half)
    ang = pos.astype(F32)[:, None] * inv_freq[None, :]
    cos, sin = jnp.cos(ang), jnp.sin(ang)
    return jnp.concatenate([cos, cos], axis=-1), jnp.concatenate([-sin, sin], axis=-1)


def _dest_tiles(dest, tm):
    k, n = dest.shape
    return dest.reshape(k, n // tm, tm).transpose(1, 0, 2)


def _layer(xp, xs, state_ret, cache_k, cache_v, cache_logf, w_in, b_fgate, ret_gn_g, w_out,
           ln1_g, ln1_b, w_router, e_bias, w_e_gate, w_e_up, w_e_down, w_s_gate, w_s_up, w_s_down,
           ln2_g, ln2_b, alpha):
    bp, tp, d = xp.shape
    bs, ts, _ = xs.shape
    past = cache_k.shape[1]
    width = w_out.shape[0] // 2
    n_heads = width // HEAD_DIM
    n_exp = w_router.shape[1]

    w_ret = w_in[:, :4 * width].astype(BF16)
    w_fox = w_in[:, 4 * width:7 * width].astype(BF16)
    n_f = w_in.shape[1] - 7 * width
    w_f = jnp.pad(w_in[:, 7 * width:], ((0, 0), (0, LANES - n_f))).astype(BF16)
    b_f = jnp.pad(b_fgate, (0, LANES - n_f)).reshape(1, LANES)
    gn = ret_gn_g.reshape(1, width)
    w_o = w_out.astype(BF16)
    l1g, l1b = ln1_g.reshape(1, d), ln1_b.reshape(1, d)
    l2g, l2b = ln2_g.reshape(1, d), ln2_b.reshape(1, d)
    wr_t = w_router.T
    wr_hi = wr_t.astype(BF16)
    wr_lo = (wr_t - wr_hi.astype(F32)).astype(BF16)
    eb = e_bias.reshape(n_exp, 1)
    weg, weu, wed = w_e_gate.astype(BF16), w_e_up.astype(BF16), w_e_down.astype(BF16)
    wsg, wsu, wsd = w_s_gate.astype(BF16), w_s_up.astype(BF16), w_s_down.astype(BF16)

    xp2 = xp.reshape(bp * tp, d)
    tm_p = _tile(tp, 512)
    cs_p, sn_p = _rope_tables(jnp.arange(tp))
    pr = _proj_ret(xp2, w_ret, cs_p, sn_p, tp, tm_p).reshape(bp, tp, 4 * width)
    fq, fk, fv, fkb, fvb, lf = _proj_fox(xp2, w_fox, w_f, b_f, tm_p)
    s0 = jnp.zeros((bp, n_heads, HEAD_DIM, HEAD_DIM), F32)
    ry_p, sfin_p = _retention(pr, s0, gn, _tile(tp, 256))
    lf_p = lf.reshape(bp, tp, LANES)
    c_p = _cumsum(lf_p, _tile(tp, 256))
    crow_p = c_p[:, :, :n_heads].transpose(0, 2, 1).reshape(bp, n_heads, 1, tp)
    tq_p = _tile(tp, 1024)
    fo_p = _fox(fq.reshape(bp, tp, width), fkb.reshape(bp, tp, width), fvb.reshape(bp, tp, width),
                c_p, crow_p, 0, tq_p, _tile(tq_p, 512))
    h_p, hp_p = _finish(xp2, ry_p.reshape(bp * tp, width), fo_p.reshape(bp * tp, width), w_o, l1g, l1b,
                        alpha, _tile(bp * tp, 256))

    xs2 = xs.reshape(bs * ts, d)
    tm_s = _tile(ts, 512)
    cs_s, sn_s = _rope_tables(past + jnp.arange(ts))
    prs = _proj_ret(xs2, w_ret, cs_s, sn_s, ts, tm_s).reshape(bs, ts, 4 * width)
    fq_s, fk_s, fv_s, fkb_s, fvb_s, lf_sn = _proj_fox(xs2, w_fox, w_f, b_f, tm_s)
    ry_s, sfin_s = _retention(prs, state_ret.astype(F32), gn, ts)
    tk_s = LANES
    t_all = -(-(past + ts) // tk_s) * tk_s
    pad_t = t_all - past - ts
    lf_s3 = lf_sn.reshape(bs, ts, LANES)
    lf_all = jnp.concatenate([
        jnp.pad(cache_logf.astype(F32), ((0, 0), (0, 0), (0, LANES - n_heads))),
        lf_s3, jnp.zeros((bs, pad_t, LANES), F32)], axis=1)
    c_s = _cumsum(lf_all, tk_s)
    crow_s = c_s[:, :, :n_heads].transpose(0, 2, 1).reshape(bs, n_heads, 1, t_all)
    zpad = jnp.zeros((bs, pad_t, width), BF16)
    k_all = jnp.concatenate([cache_k.reshape(bs, past, width).astype(BF16),
                             fkb_s.reshape(bs, ts, width), zpad], axis=1)
    v_all = jnp.concatenate([cache_v.reshape(bs, past, width).astype(BF16),
                             fvb_s.reshape(bs, ts, width), zpad], axis=1)
    fo_s = _fox(fq_s.reshape(bs, ts, width), k_all, v_all, c_s, crow_s, past, ts, tk_s)
    h_s, hp_s = _finish(xs2, ry_s.reshape(bs * ts, width), fo_s.reshape(bs * ts, width), w_o, l1g, l1b,
                        alpha, _tile(bs * ts, 256))

    n_p, n_s = bp * tp, bs * ts
    tm_r = 256
    cnt0 = jnp.zeros((n_exp, LANES), I32)
    idx_p, gate_p, rank_p, cnt1 = _router(h_p, wr_hi, wr_lo, eb, cnt0, _tile(n_p, tm_r))
    idx_s, gate_s, rank_s, cnt2 = _router(h_s, wr_hi, wr_lo, eb, cnt1, _tile(n_s, tm_r))
    counts = cnt2[:, 0]
    padded = (counts + EXPERT_ROWS - 1) // EXPERT_ROWS * EXPERT_ROWS
    pend = jnp.cumsum(padded)
    pstart = pend - padded
    dest_p = pstart[idx_p] + rank_p
    dest_s = pstart[idx_s] + rank_s
    n_blocks = -(-((n_p + n_s) * TOP_K) // EXPERT_ROWS) + n_exp
    n_used = (pend[-1] // EXPERT_ROWS).astype(I32).reshape(1)
    block_e = jnp.minimum(
        jnp.searchsorted(pend, jnp.arange(n_blocks, dtype=I32) * EXPERT_ROWS, side='right'),
        n_exp - 1).astype(I32)
    tm_d = 256
    dt_p = _dest_tiles(dest_p, _tile(n_p, tm_d))
    dt_s = _dest_tiles(dest_s, _tile(n_s, tm_d))
    xs_rows = jnp.zeros((n_blocks * EXPERT_ROWS, d // 2), U32)
    xs_rows = _dispatch(dt_p, hp_p, xs_rows, _tile(n_p, tm_d))
    xs_rows = _dispatch(dt_s, hp_s, xs_rows, _tile(n_s, tm_d))
    ys_rows = _experts(block_e, n_used, xs_rows, weg, weu, wed)
    tm_c = 128
    y_p = _combine(_dest_tiles(dest_p, _tile(n_p, tm_c)), h_p, gate_p.T, ys_rows, wsg, wsu, wsd, l2g, l2b,
                   alpha, _tile(n_p, tm_c))
    y_s = _combine(_dest_tiles(dest_s, _tile(n_s, tm_c)), h_s, gate_s.T, ys_rows, wsg, wsu, wsd, l2g, l2b,
                   alpha, _tile(n_s, tm_c))

    outs_p = (sfin_p, fk.reshape(bp, tp, n_heads, HEAD_DIM), fv.reshape(bp, tp, n_heads, HEAD_DIM),
              lf_p[:, :, :n_heads])
    outs_s = (sfin_s, fk_s.reshape(bs, ts, n_heads, HEAD_DIM), fv_s.reshape(bs, ts, n_heads, HEAD_DIM),
              lf_s3[:, :, :n_heads])
    return y_p.reshape(bp, tp, d), y_s.reshape(bs, ts, d), outs_p, outs_s


def kernel(x_prompt, x_sample, state_ret, cache_fox_k, cache_fox_v, cache_fox_logf, w_in, b_fgate, ret_gn_g, w_out, ln1_g, ln1_b, w_router, e_bias, w_e_gate, w_e_up, w_e_down, w_s_gate, w_s_up, w_s_down, ln2_g, ln2_b):
    depth = w_in.shape[0]
    alpha = (2.0 * depth) ** 0.25
    xp, xs = x_prompt, x_sample
    per_p, per_s = [], []
    for l in range(depth):
        xp, xs, op, os_ = _layer(
            xp, xs, state_ret[l], cache_fox_k[l], cache_fox_v[l], cache_fox_logf[l], w_in[l], b_fgate[l],
            ret_gn_g[l], w_out[l], ln1_g[l], ln1_b[l], w_router[l], e_bias[l], w_e_gate[l], w_e_up[l],
            w_e_down[l], w_s_gate[l], w_s_up[l], w_s_down[l], ln2_g[l], ln2_b[l], alpha)
        per_p.append(op)
        per_s.append(os_)
    stack = lambda items, j: jnp.stack([it[j] for it in items])
    return (xp, xs,
            stack(per_p, 0), stack(per_p, 1), stack(per_p, 2), stack(per_p, 3),
            stack(per_s, 0).astype(state_ret.dtype), stack(per_s, 1), stack(per_s, 2),
            stack(per_s, 3).astype(cache_fox_logf.dtype))
```

```python
import functools
import math

import jax
import jax.numpy as jnp
from jax import lax
from jax.experimental import pallas as pl
from jax.experimental.pallas import tpu as pltpu

HEAD_DIM = 128
ROPE_BASE = 10000.0
N_GROUPS = 8
TOPK_GROUPS = 4
TOP_K = 8
ROUTED_SCALE = 2.5
LN_EPS = 1e-5
LANES = 128
EXPERT_ROWS = 256
LOG2_E = math.log2(math.e)
VMEM_LIMIT_BYTES = 56 * 1024 * 1024

F32 = jnp.float32
BF16 = jnp.bfloat16
U32 = jnp.uint32
I32 = jnp.int32


def _cparams(sem):
    return pltpu.CompilerParams(dimension_semantics=sem, vmem_limit_bytes=VMEM_LIMIT_BYTES)


def _tile(n, pref):
    t = min(n, pref)
    while n % t:
        t -= 1
    return t


def _dot(a, b):
    return jnp.dot(a, b, preferred_element_type=F32)


def _dot_nt(a, b):
    return lax.dot_general(a, b, (((1,), (1,)), ((), ())), preferred_element_type=F32)


def _dot_tn(a, b):
    return lax.dot_general(a, b, (((0,), (0,)), ((), ())), preferred_element_type=F32)


def _pack_pairs(a):
    n = a.shape[1] // 2
    lo = lax.bitcast_convert_type(a[:, :n].astype(BF16).astype(F32), U32)
    hi = lax.bitcast_convert_type(a[:, n:].astype(BF16).astype(F32), U32)
    return (lo >> 16) | (hi & jnp.uint32(0xFFFF0000))


def _unpack_pairs(u):
    lo = lax.bitcast_convert_type(u << 16, F32)
    hi = lax.bitcast_convert_type(u & jnp.uint32(0xFFFF0000), F32)
    return lo, hi


def _layer_norm(z, g, b):
    mu = jnp.mean(z, axis=-1, keepdims=True)
    zc = z - mu
    var = jnp.mean(zc * zc, axis=-1, keepdims=True)
    return zc * lax.rsqrt(var + LN_EPS) * g + b


def _silu(g):
    return g * jax.nn.sigmoid(g)


def _proj_ret_kernel(x_ref, w_ref, cs_ref, sn_ref, o_ref, *, width, scale):
    xb = x_ref[...].astype(BF16)
    cs = cs_ref[...]
    sn = sn_ref[...]
    for sec in range(4):
        p = _dot(xb, w_ref[:, sec * width:(sec + 1) * width])
        if sec < 2:
            for h in range(width // HEAD_DIM):
                ph = p[:, h * HEAD_DIM:(h + 1) * HEAD_DIM]
                r = ph * cs + pltpu.roll(ph, HEAD_DIM // 2, 1) * sn
                if sec == 1:
                    r = r * scale
                o_ref[:, sec * width + h * HEAD_DIM:sec * width + (h + 1) * HEAD_DIM] = r.astype(BF16)
        else:
            o_ref[:, sec * width:(sec + 1) * width] = p.astype(BF16)


def _proj_ret(x, w, cs, sn, seq, tm):
    n, d = x.shape
    width = w.shape[1] // 4
    nt_seq = seq // tm
    return pl.pallas_call(
        functools.partial(_proj_ret_kernel, width=width, scale=HEAD_DIM ** -0.5),
        grid=(n // tm,),
        in_specs=[
            pl.BlockSpec((tm, d), lambda i: (i, 0)),
            pl.BlockSpec((d, 4 * width), lambda i: (0, 0)),
            pl.BlockSpec((tm, HEAD_DIM), lambda i: (i % nt_seq, 0)),
            pl.BlockSpec((tm, HEAD_DIM), lambda i: (i % nt_seq, 0)),
        ],
        out_specs=pl.BlockSpec((tm, 4 * width), lambda i: (i, 0)),
        out_shape=jax.ShapeDtypeStruct((n, 4 * width), BF16),
        compiler_params=_cparams(("arbitrary",)),
    )(x, w, cs, sn)


def _proj_fox_kernel(x_ref, w_ref, wf_ref, bf_ref, q_ref, k_ref, v_ref, kb_ref, vb_ref, lf_ref,
                     *, width, scale):
    xb = x_ref[...].astype(BF16)
    q = _dot(xb, w_ref[:, :width])
    q_ref[...] = (q * scale).astype(BF16)
    k = _dot(xb, w_ref[:, width:2 * width])
    k_ref[...] = k
    kb_ref[...] = k.astype(BF16)
    v = _dot(xb, w_ref[:, 2 * width:])
    v_ref[...] = v
    vb_ref[...] = v.astype(BF16)
    z = _dot(xb, wf_ref[...]) + bf_ref[...]
    lf_ref[...] = jnp.minimum(z, 0.0) - jnp.log1p(jnp.exp(-jnp.abs(z)))


def _proj_fox(x, w, wf, bfg, tm):
    n, d = x.shape
    width = w.shape[1] // 3
    row = lambda c: pl.BlockSpec((tm, c), lambda i: (i, 0))
    return pl.pallas_call(
        functools.partial(_proj_fox_kernel, width=width, scale=HEAD_DIM ** -0.5 * LOG2_E),
        grid=(n // tm,),
        in_specs=[
            row(d),
            pl.BlockSpec((d, 3 * width), lambda i: (0, 0)),
            pl.BlockSpec((d, LANES), lambda i: (0, 0)),
            pl.BlockSpec((1, LANES), lambda i: (0, 0)),
        ],
        out_specs=[row(width), row(width), row(width), row(width), row(width), row(LANES)],
        out_shape=[
            jax.ShapeDtypeStruct((n, width), BF16),
            jax.ShapeDtypeStruct((n, width), F32),
            jax.ShapeDtypeStruct((n, width), F32),
            jax.ShapeDtypeStruct((n, width), BF16),
            jax.ShapeDtypeStruct((n, width), BF16),
            jax.ShapeDtypeStruct((n, LANES), F32),
        ],
        compiler_params=_cparams(("arbitrary",)),
    )(x, w, wf, bfg)


def _cumsum_kernel(x_ref, o_ref, carry, *, tm):
    @pl.when(pl.program_id(1) == 0)
    def _():
        carry[...] = jnp.zeros_like(carry)

    x = x_ref[0]
    r = lax.broadcasted_iota(I32, (tm, tm), 0)
    c = lax.broadcasted_iota(I32, (tm, tm), 1)
    tri = jnp.where(c <= r, 1.0, 0.0).astype(BF16)
    hi = x.astype(BF16)
    r1 = x - hi.astype(F32)
    mid = r1.astype(BF16)
    lo = (r1 - mid.astype(F32)).astype(BF16)
    out = _dot(tri, hi) + _dot(tri, mid) + _dot(tri, lo) + carry[...]
    o_ref[0] = out
    carry[...] = out[tm - 1:tm, :]


def _cumsum(x, tm):
    b, t, _ = x.shape
    return pl.pallas_call(
        functools.partial(_cumsum_kernel, tm=tm),
        grid=(b, t // tm),
        in_specs=[pl.BlockSpec((1, tm, LANES), lambda i, j: (i, j, 0))],
        out_specs=pl.BlockSpec((1, tm, LANES), lambda i, j: (i, j, 0)),
        out_shape=jax.ShapeDtypeStruct(x.shape, F32),
        scratch_shapes=[pltpu.VMEM((1, LANES), F32)],
        compiler_params=_cparams(("arbitrary", "arbitrary")),
    )(x)


def _retention_kernel(q_ref, k_ref, v_ref, g_ref, s0_ref, gn_ref, y_ref, sout_ref, s_scr,
                      *, n_heads, chunk):
    c = pl.program_id(1)

    @pl.when(c == 0)
    def _():
        s_scr[...] = s0_ref[0]

    row = lax.broadcasted_iota(I32, (chunk, chunk), 0)
    col = lax.broadcasted_iota(I32, (chunk, chunk), 1)
    rel = (row - col).astype(F32)
    ri = lax.broadcasted_iota(I32, (chunk, HEAD_DIM), 0).astype(F32)
    for h in range(n_heads):
        sl = slice(h * HEAD_DIM, (h + 1) * HEAD_DIM)
        lg = math.log1p(-(2.0 ** (-5 - h)))
        decay = jnp.where(rel >= 0, jnp.exp(lg * jnp.maximum(rel, 0.0)), 0.0)
        q = q_ref[0, :, sl]
        k = k_ref[0, :, sl]
        v = v_ref[0, :, sl]
        state = s_scr[h]
        scores = _dot_nt(q, k) * decay
        o = _dot(scores.astype(BF16), v)
        o = o + jnp.exp(lg * (ri + 1.0)) * _dot(q, state.astype(BF16))
        kd = (k.astype(F32) * jnp.exp(lg * (chunk - 1.0 - ri))).astype(BF16)
        s_scr[h] = math.exp(lg * chunk) * state + _dot_tn(kd, v)
        mu = jnp.mean(o, axis=-1, keepdims=True)
        oc = o - mu
        var = jnp.mean(oc * oc, axis=-1, keepdims=True)
        yn = oc * lax.rsqrt(var + LN_EPS) * gn_ref[:, sl]
        y_ref[0, :, sl] = (_silu(g_ref[0, :, sl].astype(F32)) * yn).astype(BF16)

    @pl.when(c == pl.num_programs(1) - 1)
    def _():
        sout_ref[0] = s_scr[...]


def _retention(p, s0, gn, chunk):
    b, t, w4 = p.shape
    width = w4 // 4
    n_heads = width // HEAD_DIM
    sec = lambda s: pl.BlockSpec((1, chunk, width), lambda i, j: (i, j, s))
    st = pl.BlockSpec((1, n_heads, HEAD_DIM, HEAD_DIM), lambda i, j: (i, 0, 0, 0))
    return pl.pallas_call(
        functools.partial(_retention_kernel, n_heads=n_heads, chunk=chunk),
        grid=(b, t // chunk),
        in_specs=[sec(0), sec(1), sec(2), sec(3), st, pl.BlockSpec((1, width), lambda i, j: (0, 0))],
        out_specs=[pl.BlockSpec((1, chunk, width), lambda i, j: (i, j, 0)), st],
        out_shape=[jax.ShapeDtypeStruct((b, t, width), BF16),
                   jax.ShapeDtypeStruct(s0.shape, F32)],
        scratch_shapes=[pltpu.VMEM((n_heads, HEAD_DIM, HEAD_DIM), F32)],
        compiler_params=_cparams(("arbitrary", "arbitrary")),
    )(p, p, p, p, s0, gn)


def _fox_kernel(q_ref, k_ref, vt_ref, o_ref, *, tq, tk, q_off, n_kblocks, n_chains):
    qi = pl.program_id(2)
    tqc = tq // n_chains

    def absorb(s, q_first, vt, k0, carry, masked):
        m, l, acc = carry
        if masked:
            kpos = k0 + lax.broadcasted_iota(I32, (tk, tqc), 0)
            qpos = q_first + lax.broadcasted_iota(I32, (tk, tqc), 1)
            s = jnp.where(kpos <= qpos, s, -jnp.inf)
        m_new = jnp.maximum(m, jnp.max(s, axis=0, keepdims=True))
        alpha = jnp.exp2(m - m_new)
        p = jnp.exp2(s - m_new)
        l = alpha * l + jnp.sum(p, axis=0, keepdims=True)
        acc = alpha * acc + _dot(vt, p.astype(BF16))
        return m_new, l, acc

    def kv_block(j):
        k0 = pl.multiple_of(j * tk, tk)
        return k_ref[0, 0, pl.ds(k0, tk), :], vt_ref[0, 0, :, pl.ds(k0, tk)], k0

    init = (jnp.full((1, tqc), -jnp.inf, F32), jnp.zeros((1, tqc), F32), jnp.zeros((HEAD_DIM, tqc), F32))

    if n_chains == 1:
        q = q_ref[0, 0]
        q_first = q_off + qi * tq
        n_full = jnp.minimum((q_first + 1) // tk, n_kblocks)
        n_tot = jnp.minimum((q_first + tq + tk - 1) // tk, n_kblocks)

        def step(j, carry, masked):
            k, vt, k0 = kv_block(j)
            return absorb(_dot_nt(k, q), q_first, vt, k0, carry, masked)

        carry = lax.fori_loop(0, n_full, functools.partial(step, masked=False), init)
        _, l, acc = lax.fori_loop(n_full, n_tot, functools.partial(step, masked=True), carry)
        o_ref[0] = (acc / l).T.astype(BF16)
    else:
        qs = [q_ref[0, 0, c * tqc:(c + 1) * tqc, :] for c in range(n_chains)]
        firsts = [qi * tq + c * tqc for c in range(n_chains)]

        def step(j, carries):
            k, vt, k0 = kv_block(j)
            scores = [_dot_nt(k, qs[c]) for c in range(n_chains)]
            return tuple(absorb(scores[c], firsts[c], vt, k0, carries[c], False) for c in range(n_chains))

        carries = list(lax.fori_loop(0, qi * n_chains, step, (init,) * n_chains))
        for jj in range(n_chains):
            k, vt, k0 = kv_block(qi * n_chains + jj)
            scores = {c: _dot_nt(k, qs[c]) for c in range(jj, n_chains)}
            for c in range(jj, n_chains):
                carries[c] = absorb(scores[c], firsts[c], vt, k0, carries[c], c == jj)
        for c in range(n_chains):
            _, l, acc = carries[c]
            o_ref[0, c * tqc:(c + 1) * tqc, :] = (acc / l).T.astype(BF16)


def _fox(qa, ka, vt, q_off, tq, tk, n_chains=1):
    b, n_heads, t_q, da = qa.shape
    t_k = ka.shape[2]
    assert n_chains == 1 or (q_off == 0 and tq == n_chains * tk and t_q == t_k)
    return pl.pallas_call(
        functools.partial(_fox_kernel, tq=tq, tk=tk, q_off=q_off, n_kblocks=t_k // tk, n_chains=n_chains),
        grid=(b, n_heads, t_q // tq),
        in_specs=[
            pl.BlockSpec((1, 1, tq, da), lambda i, h, j: (i, h, j, 0)),
            pl.BlockSpec((1, 1, t_k, da), lambda i, h, j: (i, h, 0, 0)),
            pl.BlockSpec((1, 1, HEAD_DIM, t_k), lambda i, h, j: (i, h, 0, 0)),
        ],
        out_specs=pl.BlockSpec((1, tq, HEAD_DIM), lambda i, h, j: (i, j, h)),
        out_shape=jax.ShapeDtypeStruct((b, t_q, n_heads * HEAD_DIM), BF16),
        compiler_params=_cparams(("arbitrary", "arbitrary", "arbitrary")),
    )(qa, ka, vt)


def _truncate_to_bf16(x):
    bits = lax.bitcast_convert_type(x, U32) & jnp.uint32(0xFFFF0000)
    return lax.bitcast_convert_type(bits, F32)


def _fox_operands(fq, fkb, fvb, c, q_rows, q_off):
    b, t_k, width = fkb.shape
    n_heads = width // HEAD_DIM
    t_q = fq.shape[1]
    c2 = (c[:, :, :n_heads] * LOG2_E).transpose(0, 2, 1)
    hi = _truncate_to_bf16(c2)
    r1 = c2 - hi
    mid = _truncate_to_bf16(r1)
    terms = jnp.stack([hi, mid, r1 - mid], axis=-1).astype(BF16)
    ones = jnp.ones_like(terms)
    fill = jnp.zeros((b, n_heads, t_k, HEAD_DIM - 6), BF16)
    heads = lambda a: a.reshape(b, a.shape[1], n_heads, HEAD_DIM).transpose(0, 2, 1, 3)
    ka = jnp.concatenate([heads(fkb), ones, -terms, fill], axis=-1)
    qh = jnp.pad(heads(fq), ((0, 0), (0, 0), (0, q_rows - t_q), (0, 0)))
    qa = jnp.concatenate([qh, terms[:, :, q_off:q_off + q_rows], ones[:, :, :q_rows], fill[:, :, :q_rows]],
                         axis=-1)
    vt = fvb.reshape(b, t_k, n_heads, HEAD_DIM).transpose(0, 2, 3, 1)
    return qa, ka, vt


def _finish_kernel(x_ref, ry_ref, fo_ref, w_ref, g_ref, b_ref, h_ref, hp_ref, *, alpha, half):
    mix = _dot(ry_ref[...], w_ref[:half, :]) + _dot(fo_ref[...], w_ref[half:, :])
    h = _layer_norm(alpha * x_ref[...] + mix, g_ref[...], b_ref[...])
    h_ref[...] = h
    hp_ref[...] = _pack_pairs(h)


def _finish(x, ry, fo, w, g, b, alpha, tm):
    n, d = x.shape
    half = ry.shape[1]
    row = lambda c: pl.BlockSpec((tm, c), lambda i: (i, 0))
    vec = pl.BlockSpec((1, d), lambda i: (0, 0))
    return pl.pallas_call(
        functools.partial(_finish_kernel, alpha=alpha, half=half),
        grid=(n // tm,),
        in_specs=[row(d), row(half), row(fo.shape[1]), pl.BlockSpec(w.shape, lambda i: (0, 0)), vec, vec],
        out_specs=[row(d), row(d // 2)],
        out_shape=[jax.ShapeDtypeStruct((n, d), F32), jax.ShapeDtypeStruct((n, d // 2), U32)],
        compiler_params=_cparams(("arbitrary",)),
    )(x, ry, fo, w, g, b)


def _router_kernel(h_ref, whi_ref, wlo_ref, eb_ref, c0_ref, idx_ref, gate_ref, rank_ref, cnt_ref, carry,
                   *, n_exp, tm):
    @pl.when(pl.program_id(0) == 0)
    def _():
        carry[...] = c0_ref[...].astype(F32)

    h = h_ref[...]
    hhi = h.astype(BF16)
    hlo = (h - hhi.astype(F32)).astype(BF16)
    whi = whi_ref[...]
    logits = _dot_nt(whi, hhi) + _dot_nt(whi, hlo) + _dot_nt(wlo_ref[...], hhi)
    scores = jax.nn.sigmoid(logits)
    sel = scores + eb_ref[...]
    gsz = n_exp // N_GROUPS
    eio = lax.broadcasted_iota(I32, (n_exp, tm), 0).astype(F32)
    gio = lax.broadcasted_iota(I32, (gsz, tm), 0).astype(F32)
    gs_rows = []
    for g in range(N_GROUPS):
        sg = sel[g * gsz:(g + 1) * gsz]
        m1 = jnp.max(sg, axis=0, keepdims=True)
        i1 = jnp.min(jnp.where(sg == m1, gio, float(gsz)), axis=0, keepdims=True)
        m2 = jnp.max(jnp.where(gio == i1, -jnp.inf, sg), axis=0, keepdims=True)
        gs_rows.append(m1 + m2)
    gs = jnp.concatenate(gs_rows, axis=0)
    grow = lax.broadcasted_iota(I32, (N_GROUPS, tm), 0)
    beaten = jnp.zeros((N_GROUPS, tm), F32)
    for g2 in range(N_GROUPS):
        o = gs_rows[g2]
        beats = jnp.where(o > gs, 1.0, jnp.where((o == gs) & (grow > g2), 1.0, 0.0))
        beaten = beaten + beats
    gkeep = jnp.where(beaten < float(TOPK_GROUPS), 1.0, 0.0)
    selm = jnp.concatenate(
        [jnp.where(gkeep[g:g + 1] > 0.5, sel[g * gsz:(g + 1) * gsz], -jnp.inf) for g in range(N_GROUPS)],
        axis=0)
    member = jnp.zeros((n_exp, tm), F32)
    idxs, gates = [], []
    for _ in range(TOP_K):
        m = jnp.max(selm, axis=0, keepdims=True)
        ik = jnp.min(jnp.where(selm == m, eio, float(n_exp)), axis=0, keepdims=True)
        hit = eio == ik
        gates.append(jnp.sum(jnp.where(hit, scores, 0.0), axis=0, keepdims=True))
        idxs.append(ik)
        selm = jnp.where(hit, -jnp.inf, selm)
        member = jnp.where(hit, 1.0, member)
    gsum = gates[0]
    for gk in gates[1:]:
        gsum = gsum + gk
    tr = lax.broadcasted_iota(I32, (tm, tm), 0)
    tc = lax.broadcasted_iota(I32, (tm, tm), 1)
    before = jnp.where(tr < tc, 1.0, 0.0).astype(BF16)
    prefix = _dot(member.astype(BF16), before) + carry[:, :1]
    ranks = [jnp.sum(jnp.where(eio == ik, prefix, 0.0), axis=0, keepdims=True) for ik in idxs]
    carry[...] = carry[...] + jnp.sum(member, axis=1, keepdims=True)
    idx_ref[...] = jnp.concatenate(idxs, axis=0).astype(I32)
    gate_ref[...] = jnp.concatenate([gk / gsum * ROUTED_SCALE for gk in gates], axis=0)
    rank_ref[...] = jnp.concatenate(ranks, axis=0).astype(I32)
    cnt_ref[...] = carry[...].astype(I32)


def _router(h, whi, wlo, eb, cnt0, tm):
    n, d = h.shape
    n_exp = whi.shape[0]
    tok = pl.BlockSpec((TOP_K, tm), lambda i: (0, i))
    full = lambda s: pl.BlockSpec(s, lambda i: (0, 0))
    return pl.pallas_call(
        functools.partial(_router_kernel, n_exp=n_exp, tm=tm),
        grid=(n // tm,),
        in_specs=[pl.BlockSpec((tm, d), lambda i: (i, 0)), full((n_exp, d)), full((n_exp, d)),
                  full((n_exp, 1)), full((n_exp, LANES))],
        out_specs=[tok, tok, tok, full((n_exp, LANES))],
        out_shape=[jax.ShapeDtypeStruct((TOP_K, n), I32), jax.ShapeDtypeStruct((TOP_K, n), F32),
                   jax.ShapeDtypeStruct((TOP_K, n), I32), jax.ShapeDtypeStruct((n_exp, LANES), I32)],
        scratch_shapes=[pltpu.VMEM((n_exp, LANES), F32)],
        compiler_params=_cparams(("arbitrary",)),
    )(h, whi, wlo, eb, cnt0)


def _dest_kernel(idx_ref, rank_ref, ps_ref, o_ref, *, n_exp, tm):
    eio = lax.broadcasted_iota(I32, (n_exp, tm), 0)
    ps = ps_ref[...]
    rows = []
    for k in range(TOP_K):
        hit = eio == idx_ref[k:k + 1, :]
        rows.append(jnp.sum(jnp.where(hit, ps, 0.0), axis=0, keepdims=True))
    o_ref[...] = jnp.concatenate(rows, axis=0).astype(I32) + rank_ref[...]


def _dest(idx, rank, pstart, tm):
    n = idx.shape[1]
    n_exp = pstart.shape[0]
    tok = pl.BlockSpec((TOP_K, tm), lambda i: (0, i))
    return pl.pallas_call(
        functools.partial(_dest_kernel, n_exp=n_exp, tm=tm),
        grid=(n // tm,),
        in_specs=[tok, tok, pl.BlockSpec((n_exp, 1), lambda i: (0, 0))],
        out_specs=tok,
        out_shape=jax.ShapeDtypeStruct((TOP_K, n), I32),
        compiler_params=_cparams(("arbitrary",)),
    )(idx, rank, pstart.astype(F32).reshape(n_exp, 1))


def _zero_pads_kernel(last_ref, has_ref, xs_ref, zbuf, sem):
    e = pl.program_id(0)
    n_exp = pl.num_programs(0)

    def zero_copy(start):
        return pltpu.make_async_copy(zbuf, xs_ref.at[pl.ds(pl.multiple_of(start, EXPERT_ROWS), EXPERT_ROWS)], sem)

    @pl.when(e == 0)
    def _():
        zbuf[...] = jnp.zeros_like(zbuf)

    @pl.when(has_ref[e] > 0)
    def _():
        zero_copy(last_ref[e]).start()

    @pl.when(e == n_exp - 1)
    def _():
        def drain(j, carry):
            @pl.when(has_ref[j] > 0)
            def _():
                zero_copy(0).wait()
            return carry
        lax.fori_loop(0, n_exp, drain, 0)


def _zero_pads(last_start, has_rows, n_rows, half):
    n_exp = last_start.shape[0]
    grid_spec = pltpu.PrefetchScalarGridSpec(
        num_scalar_prefetch=2,
        grid=(n_exp,),
        in_specs=[],
        out_specs=pl.BlockSpec(memory_space=pl.ANY),
        scratch_shapes=[pltpu.VMEM((EXPERT_ROWS, half), U32), pltpu.SemaphoreType.DMA(())],
    )
    return pl.pallas_call(
        _zero_pads_kernel,
        grid_spec=grid_spec,
        out_shape=jax.ShapeDtypeStruct((n_rows, half), U32),
        compiler_params=_cparams(("arbitrary",)),
    )(last_start, has_rows)


def _dispatch_kernel(dest_ref, hp_ref, xs_in_ref, xs_ref, dsm, sem_d, sem, *, tm):
    del xs_in_ref
    i = pl.program_id(0)
    cp = pltpu.make_async_copy(dest_ref.at[i], dsm, sem_d)
    cp.start()
    cp.wait()

    def row_copy(t, d):
        return pltpu.make_async_copy(hp_ref.at[pl.ds(t, 1)], xs_ref.at[pl.ds(d, 1)], sem)

    def issue(t, carry):
        for k in range(TOP_K):
            row_copy(t, dsm[k, t]).start()
        return carry

    lax.fori_loop(0, tm, issue, 0)

    def drain(t, carry):
        for k in range(TOP_K):
            row_copy(0, 0).wait()
        return carry

    lax.fori_loop(0, tm, drain, 0)


def _dispatch(dest_tiles, hp, xs, tm):
    n, half = hp.shape
    return pl.pallas_call(
        functools.partial(_dispatch_kernel, tm=tm),
        grid=(n // tm,),
        in_specs=[pl.BlockSpec(memory_space=pl.ANY),
                  pl.BlockSpec((tm, half), lambda i: (i, 0)),
                  pl.BlockSpec(memory_space=pl.ANY)],
        out_specs=pl.BlockSpec(memory_space=pl.ANY),
        out_shape=jax.ShapeDtypeStruct(xs.shape, U32),
        scratch_shapes=[pltpu.SMEM((TOP_K, tm), I32), pltpu.SemaphoreType.DMA(()), pltpu.SemaphoreType.DMA(())],
        input_output_aliases={2: 0},
        compiler_params=_cparams(("arbitrary",)),
    )(dest_tiles, hp, xs)


def _experts_kernel(be_ref, nu_ref, x_ref, wg_ref, wu_ref, wd_ref, y_ref, wg_s, wu_s, wd_s, *, half):
    b = pl.program_id(0)
    active = b < nu_ref[0]
    new_expert = (b == 0) | (be_ref[b] != be_ref[jnp.maximum(b - 1, 0)])

    @pl.when(active & new_expert)
    def _():
        wg_s[...] = wg_ref[0].astype(BF16)
        wu_s[...] = wu_ref[0].astype(BF16)
        wd_s[...] = wd_ref[0].astype(BF16)

    @pl.when(active)
    def _():
        lo, hi = _unpack_pairs(x_ref[...])
        lo = lo.astype(BF16)
        hi = hi.astype(BF16)
        g = _dot(lo, wg_s[:half, :]) + _dot(hi, wg_s[half:, :])
        u = _dot(lo, wu_s[:half, :]) + _dot(hi, wu_s[half:, :])
        hm = (_silu(g) * u).astype(BF16)
        y_ref[...] = _pack_pairs(_dot(hm, wd_s[...]))


def _experts(block_e, n_used, xs, wg, wu, wd):
    n_rows, half = xs.shape
    n_exp, d, de = wg.shape
    nb = n_rows // EXPERT_ROWS
    blk = lambda b, be, nu: (jnp.minimum(b, nu[0] - 1), 0)
    wsel = lambda b, be, nu: (be[jnp.minimum(b, nu[0] - 1)], 0, 0)
    grid_spec = pltpu.PrefetchScalarGridSpec(
        num_scalar_prefetch=2,
        grid=(nb,),
        in_specs=[pl.BlockSpec((EXPERT_ROWS, half), blk),
                  pl.BlockSpec((1, d, de), wsel),
                  pl.BlockSpec((1, d, de), wsel),
                  pl.BlockSpec((1, de, d), wsel)],
        out_specs=pl.BlockSpec((EXPERT_ROWS, half), blk),
        scratch_shapes=[pltpu.VMEM((d, de), BF16), pltpu.VMEM((d, de), BF16), pltpu.VMEM((de, d), BF16)],
    )
    return pl.pallas_call(
        functools.partial(_experts_kernel, half=half),
        grid_spec=grid_spec,
        out_shape=jax.ShapeDtypeStruct((n_rows, half), U32),
        compiler_params=_cparams(("arbitrary",)),
    )(block_e, n_used, xs, wg, wu, wd)


def _combine_kernel(dest_ref, h_ref, gate_ref, ys_ref, wsg_ref, wsu_ref, wsd_ref, g_ref, b_ref, y_ref,
                    buf, dsm, sem_d, sem, *, tm, alpha):
    i = pl.program_id(0)
    cp = pltpu.make_async_copy(dest_ref.at[i], dsm, sem_d)
    cp.start()
    cp.wait()

    def row_copy(k, t, d):
        return pltpu.make_async_copy(ys_ref.at[pl.ds(d, 1)], buf.at[k, pl.ds(t, 1)], sem)

    def issue(t, carry):
        for k in range(TOP_K):
            row_copy(k, t, dsm[k, t]).start()
        return carry

    lax.fori_loop(0, tm, issue, 0)

    h = h_ref[...]
    hb = h.astype(BF16)
    sh = _dot((_silu(_dot(hb, wsg_ref[...])) * _dot(hb, wsu_ref[...])).astype(BF16), wsd_ref[...])

    def drain(t, carry):
        for k in range(TOP_K):
            row_copy(0, 0, 0).wait()
        return carry

    lax.fori_loop(0, tm, drain, 0)

    gate = gate_ref[...]
    acc_lo = jnp.zeros((tm, buf.shape[2]), F32)
    acc_hi = jnp.zeros((tm, buf.shape[2]), F32)
    for k in range(TOP_K):
        lo, hi = _unpack_pairs(buf[k])
        gk = gate[:, k:k + 1]
        acc_lo = acc_lo + gk * lo
        acc_hi = acc_hi + gk * hi
    routed = jnp.concatenate([acc_lo, acc_hi], axis=1)
    y_ref[...] = _layer_norm(alpha * h + (routed + sh), g_ref[...], b_ref[...])


def _combine(dest_tiles, h, gate_t, ys, wsg, wsu, wsd, g, b, alpha, tm):
    n, d = h.shape
    half = ys.shape[1]
    full = lambda a: pl.BlockSpec(a.shape, lambda i: (0, 0))
    return pl.pallas_call(
        functools.partial(_combine_kernel, tm=tm, alpha=alpha),
        grid=(n // tm,),
        in_specs=[pl.BlockSpec(memory_space=pl.ANY),
                  pl.BlockSpec((tm, d), lambda i: (i, 0)),
                  pl.BlockSpec((tm, TOP_K), lambda i: (i, 0)),
                  pl.BlockSpec(memory_space=pl.ANY),
                  full(wsg), full(wsu), full(wsd), full(g), full(b)],
        out_specs=pl.BlockSpec((tm, d), lambda i: (i, 0)),
        out_shape=jax.ShapeDtypeStruct((n, d), F32),
        scratch_shapes=[pltpu.VMEM((TOP_K, tm, half), U32), pltpu.SMEM((TOP_K, tm), I32),
                        pltpu.SemaphoreType.DMA(()), pltpu.SemaphoreType.DMA(())],
        compiler_params=_cparams(("arbitrary",)),
    )(dest_tiles, h, gate_t, ys, wsg, wsu, wsd, g, b)


def _rope_tables(pos):
    half = HEAD_DIM // 2
    inv_freq = ROPE_BASE ** (-jnp.arange(half, dtype=F32) / half)
    ang = pos.astype(F32)[:, None] * inv_freq[None, :]
    cos, sin = jnp.cos(ang), jnp.sin(ang)
    return jnp.concatenate([cos, cos], axis=-1), jnp.concatenate([-sin, sin], axis=-1)


def _dest_tiles(dest, tm):
    k, n = dest.shape
    return dest.reshape(k, n // tm, tm).transpose(1, 0, 2)


def _layer(xp, xs, state_ret, cache_k, cache_v, cache_logf, w_in, b_fgate, ret_gn_g, w_out,
           ln1_g, ln1_b, w_router, e_bias, w_e_gate, w_e_up, w_e_down, w_s_gate, w_s_up, w_s_down,
           ln2_g, ln2_b, alpha):
    bp, tp, d = xp.shape
    bs, ts, _ = xs.shape
    past = cache_k.shape[1]
    width = w_out.shape[0] // 2
    n_heads = width // HEAD_DIM
    n_exp = w_router.shape[1]

    w_ret = w_in[:, :4 * width].astype(BF16)
    w_fox = w_in[:, 4 * width:7 * width].astype(BF16)
    n_f = w_in.shape[1] - 7 * width
    w_f = jnp.pad(w_in[:, 7 * width:], ((0, 0), (0, LANES - n_f))).astype(BF16)
    b_f = jnp.pad(b_fgate, (0, LANES - n_f)).reshape(1, LANES)
    gn = ret_gn_g.reshape(1, width)
    w_o = w_out.astype(BF16)
    l1g, l1b = ln1_g.reshape(1, d), ln1_b.reshape(1, d)
    l2g, l2b = ln2_g.reshape(1, d), ln2_b.reshape(1, d)
    wr_t = w_router.T
    wr_top = _truncate_to_bf16(wr_t)
    wr_hi = wr_top.astype(BF16)
    wr_lo = (wr_t - wr_top).astype(BF16)
    eb = e_bias.reshape(n_exp, 1)
    wsg, wsu, wsd = w_s_gate.astype(BF16), w_s_up.astype(BF16), w_s_down.astype(BF16)

    xp2 = xp.reshape(bp * tp, d)
    tm_p = _tile(tp, 512)
    cs_p, sn_p = _rope_tables(jnp.arange(tp))
    pr = _proj_ret(xp2, w_ret, cs_p, sn_p, tp, tm_p).reshape(bp, tp, 4 * width)
    fq, fk, fv, fkb, fvb, lf = _proj_fox(xp2, w_fox, w_f, b_f, tm_p)
    s0 = jnp.zeros((bp, n_heads, HEAD_DIM, HEAD_DIM), F32)
    ry_p, sfin_p = _retention(pr, s0, gn, _tile(tp, 256))
    lf_p = lf.reshape(bp, tp, LANES)
    c_p = _cumsum(lf_p, _tile(tp, 256))
    qa_p, ka_p, vt_p = _fox_operands(fq.reshape(bp, tp, width), fkb.reshape(bp, tp, width),
                                     fvb.reshape(bp, tp, width), c_p, tp, 0)
    tq_p = _tile(tp, 2048)
    fo_p = _fox(qa_p, ka_p, vt_p, 0, tq_p, tq_p // 2, n_chains=2)
    h_p, hp_p = _finish(xp2, ry_p.reshape(bp * tp, width), fo_p.reshape(bp * tp, width), w_o, l1g, l1b,
                        alpha, _tile(bp * tp, 256))

    xs2 = xs.reshape(bs * ts, d)
    tm_s = _tile(ts, 512)
    cs_s, sn_s = _rope_tables(past + jnp.arange(ts))
    prs = _proj_ret(xs2, w_ret, cs_s, sn_s, ts, tm_s).reshape(bs, ts, 4 * width)
    fq_s, fk_s, fv_s, fkb_s, fvb_s, lf_sn = _proj_fox(xs2, w_fox, w_f, b_f, tm_s)
    ry_s, sfin_s = _retention(prs, state_ret.astype(F32), gn, ts)
    tk_s = LANES
    tq_s = -(-ts // LANES) * LANES
    t_all = -(-(past + tq_s) // tk_s) * tk_s
    pad_t = t_all - past - ts
    lf_s3 = lf_sn.reshape(bs, ts, LANES)
    lf_all = jnp.concatenate([
        jnp.pad(cache_logf.astype(F32), ((0, 0), (0, 0), (0, LANES - n_heads))),
        lf_s3, jnp.zeros((bs, pad_t, LANES), F32)], axis=1)
    c_s = _cumsum(lf_all, tk_s)
    zpad = jnp.zeros((bs, pad_t, width), BF16)
    k_all = jnp.concatenate([cache_k.reshape(bs, past, width).astype(BF16),
                             fkb_s.reshape(bs, ts, width), zpad], axis=1)
    v_all = jnp.concatenate([cache_v.reshape(bs, past, width).astype(BF16),
                             fvb_s.reshape(bs, ts, width), zpad], axis=1)
    qa_s, ka_s, vt_s = _fox_operands(fq_s.reshape(bs, ts, width), k_all, v_all, c_s, tq_s, past)
    fo_s = _fox(qa_s, ka_s, vt_s, past, tq_s, t_all)[:, :ts]
    h_s, hp_s = _finish(xs2, ry_s.reshape(bs * ts, width), fo_s.reshape(bs * ts, width), w_o, l1g, l1b,
                        alpha, _tile(bs * ts, 256))

    n_p, n_s = bp * tp, bs * ts
    tm_r = 256
    cnt0 = jnp.zeros((n_exp, LANES), I32)
    idx_p, gate_p, rank_p, cnt1 = _router(h_p, wr_hi, wr_lo, eb, cnt0, _tile(n_p, tm_r))
    idx_s, gate_s, rank_s, cnt2 = _router(h_s, wr_hi, wr_lo, eb, cnt1, _tile(n_s, tm_r))
    counts = cnt2[:, 0]
    padded = (counts + EXPERT_ROWS - 1) // EXPERT_ROWS * EXPERT_ROWS
    pend = jnp.cumsum(padded)
    pstart = pend - padded
    dest_p = _dest(idx_p, rank_p, pstart, _tile(n_p, 512))
    dest_s = _dest(idx_s, rank_s, pstart, _tile(n_s, 512))
    n_blocks = -(-((n_p + n_s) * TOP_K) // EXPERT_ROWS) + n_exp
    n_used = (pend[-1] // EXPERT_ROWS).astype(I32).reshape(1)
    block_start = jnp.arange(n_blocks, dtype=I32) * EXPERT_ROWS
    block_e = jnp.minimum(jnp.sum((pend[None, :] <= block_start[:, None]).astype(I32), axis=1),
                          n_exp - 1).astype(I32)
    tm_d = 256
    dt_p = _dest_tiles(dest_p, _tile(n_p, tm_d))
    dt_s = _dest_tiles(dest_s, _tile(n_s, tm_d))
    xs_rows = _zero_pads((pend - EXPERT_ROWS).astype(I32), padded.astype(I32), n_blocks * EXPERT_ROWS, d // 2)
    xs_rows = _dispatch(dt_p, hp_p, xs_rows, _tile(n_p, tm_d))
    xs_rows = _dispatch(dt_s, hp_s, xs_rows, _tile(n_s, tm_d))
    ys_rows = _experts(block_e, n_used, xs_rows, w_e_gate, w_e_up, w_e_down)
    tm_c = 128
    y_p = _combine(_dest_tiles(dest_p, _tile(n_p, tm_c)), h_p, gate_p.T, ys_rows, wsg, wsu, wsd, l2g, l2b,
                   alpha, _tile(n_p, tm_c))
    y_s = _combine(_dest_tiles(dest_s, _tile(n_s, tm_c)), h_s, gate_s.T, ys_rows, wsg, wsu, wsd, l2g, l2b,
                   alpha, _tile(n_s, tm_c))

    outs_p = (sfin_p, fk.reshape(bp, tp, n_heads, HEAD_DIM), fv.reshape(bp, tp, n_heads, HEAD_DIM),
              lf_p[:, :, :n_heads])
    outs_s = (sfin_s, fk_s.reshape(bs, ts, n_heads, HEAD_DIM), fv_s.reshape(bs, ts, n_heads, HEAD_DIM),
              lf_s3[:, :, :n_heads])
    return y_p.reshape(bp, tp, d), y_s.reshape(bs, ts, d), outs_p, outs_s


def kernel(x_prompt, x_sample, state_ret, cache_fox_k, cache_fox_v, cache_fox_logf, w_in, b_fgate, ret_gn_g, w_out, ln1_g, ln1_b, w_router, e_bias, w_e_gate, w_e_up, w_e_down, w_s_gate, w_s_up, w_s_down, ln2_g, ln2_b):
    depth = w_in.shape[0]
    alpha = (2.0 * depth) ** 0.25
    xp, xs = x_prompt, x_sample
    per_p, per_s = [], []
    for l in range(depth):
        xp, xs, op, os_ = _layer(
            xp, xs, state_ret[l], cache_fox_k[l], cache_fox_v[l], cache_fox_logf[l], w_in[l], b_fgate[l],
            ret_gn_g[l], w_out[l], ln1_g[l], ln1_b[l], w_router[l], e_bias[l], w_e_gate[l], w_e_up[l],
            w_e_down[l], w_s_gate[l], w_s_up[l], w_s_down[l], ln2_g[l], ln2_b[l], alpha)
        per_p.append(op)
        per_s.append(os_)
    stack = lambda items, j: jnp.stack([it[j] for it in items])
    return (xp, xs,
            stack(per_p, 0), stack(per_p, 1), stack(per_p, 2), stack(per_p, 3),
            stack(per_s, 0).astype(state_ret.dtype), stack(per_s, 1), stack(per_s, 2),
            stack(per_s, 3).astype(cache_fox_logf.dtype))
```

```python
import functools
import math

import jax
import jax.numpy as jnp
from jax import lax
from jax.experimental import pallas as pl
from jax.experimental.pallas import tpu as pltpu

HEAD_DIM = 128
ROPE_BASE = 10000.0
N_GROUPS = 8
TOPK_GROUPS = 4
TOP_K = 8
ROUTED_SCALE = 2.5
LN_EPS = 1e-5
LANES = 128
EXPERT_ROWS = 256
LOG2_E = math.log2(math.e)
VMEM_LIMIT_BYTES = 56 * 1024 * 1024

F32 = jnp.float32
BF16 = jnp.bfloat16
U32 = jnp.uint32
I32 = jnp.int32


def _cparams(sem):
    return pltpu.CompilerParams(dimension_semantics=sem, vmem_limit_bytes=VMEM_LIMIT_BYTES)


def _tile(n, pref):
    t = min(n, pref)
    while n % t:
        t -= 1
    return t


def _dot(a, b):
    return jnp.dot(a, b, preferred_element_type=F32)


def _dot_nt(a, b):
    return lax.dot_general(a, b, (((1,), (1,)), ((), ())), preferred_element_type=F32)


def _dot_tn(a, b):
    return lax.dot_general(a, b, (((0,), (0,)), ((), ())), preferred_element_type=F32)


def _pack_pairs(a):
    n = a.shape[1] // 2
    lo = lax.bitcast_convert_type(a[:, :n].astype(BF16).astype(F32), U32)
    hi = lax.bitcast_convert_type(a[:, n:].astype(BF16).astype(F32), U32)
    return (lo >> 16) | (hi & jnp.uint32(0xFFFF0000))


def _unpack_pairs(u):
    lo = lax.bitcast_convert_type(u << 16, F32)
    hi = lax.bitcast_convert_type(u & jnp.uint32(0xFFFF0000), F32)
    return lo, hi


def _layer_norm(z, g, b):
    mu = jnp.mean(z, axis=-1, keepdims=True)
    zc = z - mu
    var = jnp.mean(zc * zc, axis=-1, keepdims=True)
    return zc * lax.rsqrt(var + LN_EPS) * g + b


def _silu(g):
    return g * jax.nn.sigmoid(g)


def _proj_ret_kernel(x_ref, w_ref, cs_ref, sn_ref, o_ref, *, width, scale):
    xb = x_ref[...].astype(BF16)
    cs = cs_ref[...]
    sn = sn_ref[...]
    for sec in range(4):
        p = _dot(xb, w_ref[:, sec * width:(sec + 1) * width])
        if sec < 2:
            for h in range(width // HEAD_DIM):
                ph = p[:, h * HEAD_DIM:(h + 1) * HEAD_DIM]
                r = ph * cs + pltpu.roll(ph, HEAD_DIM // 2, 1) * sn
                if sec == 1:
                    r = r * scale
                o_ref[:, sec * width + h * HEAD_DIM:sec * width + (h + 1) * HEAD_DIM] = r.astype(BF16)
        else:
            o_ref[:, sec * width:(sec + 1) * width] = p.astype(BF16)


def _proj_ret(x, w, cs, sn, seq, tm):
    n, d = x.shape
    width = w.shape[1] // 4
    nt_seq = seq // tm
    return pl.pallas_call(
        functools.partial(_proj_ret_kernel, width=width, scale=HEAD_DIM ** -0.5),
        grid=(n // tm,),
        in_specs=[
            pl.BlockSpec((tm, d), lambda i: (i, 0)),
            pl.BlockSpec((d, 4 * width), lambda i: (0, 0)),
            pl.BlockSpec((tm, HEAD_DIM), lambda i: (i % nt_seq, 0)),
            pl.BlockSpec((tm, HEAD_DIM), lambda i: (i % nt_seq, 0)),
        ],
        out_specs=pl.BlockSpec((tm, 4 * width), lambda i: (i, 0)),
        out_shape=jax.ShapeDtypeStruct((n, 4 * width), BF16),
        compiler_params=_cparams(("arbitrary",)),
    )(x, w, cs, sn)


def _proj_fox_kernel(x_ref, w_ref, wf_ref, bf_ref, q_ref, k_ref, v_ref, kb_ref, vb_ref, lf_ref,
                     *, width, scale):
    xb = x_ref[...].astype(BF16)
    q = _dot(xb, w_ref[:, :width])
    q_ref[...] = (q * scale).astype(BF16)
    k = _dot(xb, w_ref[:, width:2 * width])
    k_ref[...] = k
    kb_ref[...] = k.astype(BF16)
    v = _dot(xb, w_ref[:, 2 * width:])
    v_ref[...] = v
    vb_ref[...] = v.astype(BF16)
    z = _dot(xb, wf_ref[...]) + bf_ref[...]
    lf_ref[...] = jnp.minimum(z, 0.0) - jnp.log1p(jnp.exp(-jnp.abs(z)))


def _log_sigmoid(z):
    return jnp.minimum(z, 0.0) - jnp.log1p(jnp.exp(-jnp.abs(z)))


def _proj_fox_heads_kernel(x_ref, w_ref, wf_ref, bf_ref, k_ref, v_ref, lf_ref, qa_ref, ka_ref, vt_ref, carry,
                           *, width, scale, n_heads):
    @pl.when(pl.program_id(1) == 0)
    def _():
        carry[...] = jnp.zeros_like(carry)

    tm = x_ref.shape[0]
    xb = x_ref[...].astype(BF16)
    q = _dot(xb, w_ref[:, :width]) * scale
    k = _dot(xb, w_ref[:, width:2 * width])
    v = _dot(xb, w_ref[:, 2 * width:])
    k_ref[...] = k
    v_ref[...] = v
    logf = _log_sigmoid(_dot(xb, wf_ref[...]) + bf_ref[...])
    lf_ref[...] = logf
    c = _tile_cumsum(logf, carry[...])
    carry[...] = c[tm - 1:tm, :]
    c2 = c * LOG2_E
    lane = lax.broadcasted_iota(I32, (tm, HEAD_DIM), 1)
    for h in range(n_heads):
        sl = slice(h * HEAD_DIM, (h + 1) * HEAD_DIM)
        hi, mid, lo = _split3(c2[:, h:h + 1])
        terms = jnp.where(lane == 0, hi.astype(F32), jnp.where(lane == 1, mid.astype(F32), lo.astype(F32)))
        q_tail = jnp.where(lane < 3, terms, jnp.where(lane < 6, 1.0, 0.0))
        k_tail = jnp.where(lane < 3, 1.0, jnp.where(lane < 6, -pltpu.roll(terms, 3, 1), 0.0))
        qa_ref[0, h, :, :HEAD_DIM] = q[:, sl].astype(BF16)
        qa_ref[0, h, :, HEAD_DIM:] = q_tail.astype(BF16)
        ka_ref[0, h, :, :HEAD_DIM] = k[:, sl].astype(BF16)
        ka_ref[0, h, :, HEAD_DIM:] = k_tail.astype(BF16)
        vt_ref[0, h] = v[:, sl].T.astype(BF16)


def _proj_fox_heads(x, w, wf, bfg, batch, tm):
    n, d = x.shape
    width = w.shape[1] // 3
    n_heads = width // HEAD_DIM
    seq = n // batch
    nt = seq // tm
    row = lambda c: pl.BlockSpec((tm, c), lambda b, i: (b * nt + i, 0))
    const = lambda s: pl.BlockSpec(s, lambda b, i: (0, 0))
    aug = pl.BlockSpec((1, n_heads, tm, 2 * HEAD_DIM), lambda b, i: (b, 0, i, 0))
    return pl.pallas_call(
        functools.partial(_proj_fox_heads_kernel, width=width, scale=HEAD_DIM ** -0.5 * LOG2_E,
                          n_heads=n_heads),
        grid=(batch, nt),
        in_specs=[row(d), const((d, 3 * width)), const((d, LANES)), const((1, LANES))],
        out_specs=[row(width), row(width), row(LANES), aug, aug,
                   pl.BlockSpec((1, n_heads, HEAD_DIM, tm), lambda b, i: (b, 0, 0, i))],
        out_shape=[
            jax.ShapeDtypeStruct((n, width), F32),
            jax.ShapeDtypeStruct((n, width), F32),
            jax.ShapeDtypeStruct((n, LANES), F32),
            jax.ShapeDtypeStruct((batch, n_heads, seq, 2 * HEAD_DIM), BF16),
            jax.ShapeDtypeStruct((batch, n_heads, seq, 2 * HEAD_DIM), BF16),
            jax.ShapeDtypeStruct((batch, n_heads, HEAD_DIM, seq), BF16),
        ],
        scratch_shapes=[pltpu.VMEM((1, LANES), F32)],
        compiler_params=_cparams(("arbitrary", "arbitrary")),
    )(x, w, wf, bfg)


def _proj_fox(x, w, wf, bfg, tm):
    n, d = x.shape
    width = w.shape[1] // 3
    row = lambda c: pl.BlockSpec((tm, c), lambda i: (i, 0))
    return pl.pallas_call(
        functools.partial(_proj_fox_kernel, width=width, scale=HEAD_DIM ** -0.5 * LOG2_E),
        grid=(n // tm,),
        in_specs=[
            row(d),
            pl.BlockSpec((d, 3 * width), lambda i: (0, 0)),
            pl.BlockSpec((d, LANES), lambda i: (0, 0)),
            pl.BlockSpec((1, LANES), lambda i: (0, 0)),
        ],
        out_specs=[row(width), row(width), row(width), row(width), row(width), row(LANES)],
        out_shape=[
            jax.ShapeDtypeStruct((n, width), BF16),
            jax.ShapeDtypeStruct((n, width), F32),
            jax.ShapeDtypeStruct((n, width), F32),
            jax.ShapeDtypeStruct((n, width), BF16),
            jax.ShapeDtypeStruct((n, width), BF16),
            jax.ShapeDtypeStruct((n, LANES), F32),
        ],
        compiler_params=_cparams(("arbitrary",)),
    )(x, w, wf, bfg)


def _split3(x):
    hi = x.astype(BF16)
    r1 = x - hi.astype(F32)
    mid = r1.astype(BF16)
    lo = (r1 - mid.astype(F32)).astype(BF16)
    return hi, mid, lo


def _tile_cumsum(x, carry_row):
    tm = x.shape[0]
    r = lax.broadcasted_iota(I32, (tm, tm), 0)
    c = lax.broadcasted_iota(I32, (tm, tm), 1)
    tri = jnp.where(c <= r, 1.0, 0.0).astype(BF16)
    hi, mid, lo = _split3(x)
    return _dot(tri, hi) + _dot(tri, mid) + _dot(tri, lo) + carry_row


def _cumsum_kernel(x_ref, o_ref, carry, *, tm):
    @pl.when(pl.program_id(1) == 0)
    def _():
        carry[...] = jnp.zeros_like(carry)

    out = _tile_cumsum(x_ref[0], carry[...])
    o_ref[0] = out
    carry[...] = out[tm - 1:tm, :]


def _cumsum(x, tm):
    b, t, _ = x.shape
    return pl.pallas_call(
        functools.partial(_cumsum_kernel, tm=tm),
        grid=(b, t // tm),
        in_specs=[pl.BlockSpec((1, tm, LANES), lambda i, j: (i, j, 0))],
        out_specs=pl.BlockSpec((1, tm, LANES), lambda i, j: (i, j, 0)),
        out_shape=jax.ShapeDtypeStruct(x.shape, F32),
        scratch_shapes=[pltpu.VMEM((1, LANES), F32)],
        compiler_params=_cparams(("arbitrary", "arbitrary")),
    )(x)


def _retention_kernel(q_ref, k_ref, v_ref, g_ref, s0_ref, gn_ref, y_ref, sout_ref, s_scr,
                      *, n_heads, chunk):
    c = pl.program_id(1)

    @pl.when(c == 0)
    def _():
        s_scr[...] = s0_ref[0]

    row = lax.broadcasted_iota(I32, (chunk, chunk), 0)
    col = lax.broadcasted_iota(I32, (chunk, chunk), 1)
    rel = (row - col).astype(F32)
    ri = lax.broadcasted_iota(I32, (chunk, HEAD_DIM), 0).astype(F32)
    for h in range(n_heads):
        sl = slice(h * HEAD_DIM, (h + 1) * HEAD_DIM)
        lg = math.log1p(-(2.0 ** (-5 - h)))
        decay = jnp.where(rel >= 0, jnp.exp(lg * jnp.maximum(rel, 0.0)), 0.0)
        q = q_ref[0, :, sl]
        k = k_ref[0, :, sl]
        v = v_ref[0, :, sl]
        state = s_scr[h]
        scores = _dot_nt(q, k) * decay
        o = _dot(scores.astype(BF16), v)
        o = o + jnp.exp(lg * (ri + 1.0)) * _dot(q, state.astype(BF16))
        kd = (k.astype(F32) * jnp.exp(lg * (chunk - 1.0 - ri))).astype(BF16)
        s_scr[h] = math.exp(lg * chunk) * state + _dot_tn(kd, v)
        mu = jnp.mean(o, axis=-1, keepdims=True)
        oc = o - mu
        var = jnp.mean(oc * oc, axis=-1, keepdims=True)
        yn = oc * lax.rsqrt(var + LN_EPS) * gn_ref[:, sl]
        y_ref[0, :, sl] = (_silu(g_ref[0, :, sl].astype(F32)) * yn).astype(BF16)

    @pl.when(c == pl.num_programs(1) - 1)
    def _():
        sout_ref[0] = s_scr[...]


def _retention(p, s0, gn, chunk):
    b, t, w4 = p.shape
    width = w4 // 4
    n_heads = width // HEAD_DIM
    sec = lambda s: pl.BlockSpec((1, chunk, width), lambda i, j: (i, j, s))
    st = pl.BlockSpec((1, n_heads, HEAD_DIM, HEAD_DIM), lambda i, j: (i, 0, 0, 0))
    return pl.pallas_call(
        functools.partial(_retention_kernel, n_heads=n_heads, chunk=chunk),
        grid=(b, t // chunk),
        in_specs=[sec(0), sec(1), sec(2), sec(3), st, pl.BlockSpec((1, width), lambda i, j: (0, 0))],
        out_specs=[pl.BlockSpec((1, chunk, width), lambda i, j: (i, j, 0)), st],
        out_shape=[jax.ShapeDtypeStruct((b, t, width), BF16),
                   jax.ShapeDtypeStruct(s0.shape, F32)],
        scratch_shapes=[pltpu.VMEM((n_heads, HEAD_DIM, HEAD_DIM), F32)],
        compiler_params=_cparams(("arbitrary", "arbitrary")),
    )(p, p, p, p, s0, gn)


def _fox_kernel(q_ref, k_ref, vt_ref, o_ref, *, tq, tk, q_off, n_kblocks, n_chains):
    qi = pl.program_id(2)
    tqc = tq // n_chains

    def absorb(s, q_first, vt, k0, carry, masked):
        m, l, acc = carry
        if masked:
            kpos = k0 + lax.broadcasted_iota(I32, (tk, tqc), 0)
            qpos = q_first + lax.broadcasted_iota(I32, (tk, tqc), 1)
            s = jnp.where(kpos <= qpos, s, -jnp.inf)
        m_new = jnp.maximum(m, jnp.max(s, axis=0, keepdims=True))
        alpha = jnp.exp2(m - m_new)
        p = jnp.exp2(s - m_new)
        l = alpha * l + jnp.sum(p, axis=0, keepdims=True)
        acc = alpha * acc + _dot(vt, p.astype(BF16))
        return m_new, l, acc

    def kv_block(j):
        k0 = pl.multiple_of(j * tk, tk)
        return k_ref[0, 0, pl.ds(k0, tk), :], vt_ref[0, 0, :, pl.ds(k0, tk)], k0

    init = (jnp.full((1, tqc), -jnp.inf, F32), jnp.zeros((1, tqc), F32), jnp.zeros((HEAD_DIM, tqc), F32))

    if n_chains == 1:
        q = q_ref[0, 0]
        q_first = q_off + qi * tq
        n_full = jnp.minimum((q_first + 1) // tk, n_kblocks)
        n_tot = jnp.minimum((q_first + tq + tk - 1) // tk, n_kblocks)

        def step(j, carry, masked):
            k, vt, k0 = kv_block(j)
            return absorb(_dot_nt(k, q), q_first, vt, k0, carry, masked)

        carry = lax.fori_loop(0, n_full, functools.partial(step, masked=False), init)
        _, l, acc = lax.fori_loop(n_full, n_tot, functools.partial(step, masked=True), carry)
        o_ref[0] = (acc / l).T.astype(BF16)
    else:
        qs = [q_ref[0, 0, c * tqc:(c + 1) * tqc, :] for c in range(n_chains)]
        firsts = [qi * tq + c * tqc for c in range(n_chains)]

        def step(j, carries):
            k, vt, k0 = kv_block(j)
            scores = [_dot_nt(k, qs[c]) for c in range(n_chains)]
            return tuple(absorb(scores[c], firsts[c], vt, k0, carries[c], False) for c in range(n_chains))

        carries = list(lax.fori_loop(0, qi * n_chains, step, (init,) * n_chains))
        for jj in range(n_chains):
            k, vt, k0 = kv_block(qi * n_chains + jj)
            scores = {c: _dot_nt(k, qs[c]) for c in range(jj, n_chains)}
            for c in range(jj, n_chains):
                carries[c] = absorb(scores[c], firsts[c], vt, k0, carries[c], c == jj)
        for c in range(n_chains):
            _, l, acc = carries[c]
            o_ref[0, c * tqc:(c + 1) * tqc, :] = (acc / l).T.astype(BF16)


def _fox(qa, ka, vt, q_off, tq, tk, n_chains=1):
    b, n_heads, t_q, da = qa.shape
    t_k = ka.shape[2]
    assert n_chains == 1 or (q_off == 0 and tq == n_chains * tk and t_q == t_k)
    return pl.pallas_call(
        functools.partial(_fox_kernel, tq=tq, tk=tk, q_off=q_off, n_kblocks=t_k // tk, n_chains=n_chains),
        grid=(b, n_heads, t_q // tq),
        in_specs=[
            pl.BlockSpec((1, 1, tq, da), lambda i, h, j: (i, h, j, 0)),
            pl.BlockSpec((1, 1, t_k, da), lambda i, h, j: (i, h, 0, 0)),
            pl.BlockSpec((1, 1, HEAD_DIM, t_k), lambda i, h, j: (i, h, 0, 0)),
        ],
        out_specs=pl.BlockSpec((1, tq, HEAD_DIM), lambda i, h, j: (i, j, h)),
        out_shape=jax.ShapeDtypeStruct((b, t_q, n_heads * HEAD_DIM), BF16),
        compiler_params=_cparams(("arbitrary", "arbitrary", "arbitrary")),
    )(qa, ka, vt)


def _truncate_to_bf16(x):
    bits = lax.bitcast_convert_type(x, U32) & jnp.uint32(0xFFFF0000)
    return lax.bitcast_convert_type(bits, F32)


def _fox_operands(fq, fkb, fvb, c, q_rows, q_off):
    b, t_k, width = fkb.shape
    n_heads = width // HEAD_DIM
    t_q = fq.shape[1]
    c2 = (c[:, :, :n_heads] * LOG2_E).transpose(0, 2, 1)
    hi = _truncate_to_bf16(c2)
    r1 = c2 - hi
    mid = _truncate_to_bf16(r1)
    terms = jnp.stack([hi, mid, r1 - mid], axis=-1).astype(BF16)
    ones = jnp.ones_like(terms)
    fill = jnp.zeros((b, n_heads, t_k, HEAD_DIM - 6), BF16)
    heads = lambda a: a.reshape(b, a.shape[1], n_heads, HEAD_DIM).transpose(0, 2, 1, 3)
    ka = jnp.concatenate([heads(fkb), ones, -terms, fill], axis=-1)
    qh = jnp.pad(heads(fq), ((0, 0), (0, 0), (0, q_rows - t_q), (0, 0)))
    qa = jnp.concatenate([qh, terms[:, :, q_off:q_off + q_rows], ones[:, :, :q_rows], fill[:, :, :q_rows]],
                         axis=-1)
    vt = fvb.reshape(b, t_k, n_heads, HEAD_DIM).transpose(0, 2, 3, 1)
    return qa, ka, vt


def _finish_kernel(x_ref, ry_ref, fo_ref, w_ref, g_ref, b_ref, h_ref, hp_ref, *, alpha, half):
    mix = _dot(ry_ref[...], w_ref[:half, :]) + _dot(fo_ref[...], w_ref[half:, :])
    h = _layer_norm(alpha * x_ref[...] + mix, g_ref[...], b_ref[...])
    h_ref[...] = h
    hp_ref[...] = _pack_pairs(h)


def _finish(x, ry, fo, w, g, b, alpha, tm):
    n, d = x.shape
    half = ry.shape[1]
    row = lambda c: pl.BlockSpec((tm, c), lambda i: (i, 0))
    vec = pl.BlockSpec((1, d), lambda i: (0, 0))
    return pl.pallas_call(
        functools.partial(_finish_kernel, alpha=alpha, half=half),
        grid=(n // tm,),
        in_specs=[row(d), row(half), row(fo.shape[1]), pl.BlockSpec(w.shape, lambda i: (0, 0)), vec, vec],
        out_specs=[row(d), row(d // 2)],
        out_shape=[jax.ShapeDtypeStruct((n, d), F32), jax.ShapeDtypeStruct((n, d // 2), U32)],
        compiler_params=_cparams(("arbitrary",)),
    )(x, ry, fo, w, g, b)


def _router_kernel(h_ref, whi_ref, wlo_ref, eb_ref, c0_ref, idx_ref, gate_ref, rank_ref, cnt_ref, carry,
                   *, n_exp, tm):
    @pl.when(pl.program_id(0) == 0)
    def _():
        carry[...] = c0_ref[...].astype(F32)

    h = h_ref[...]
    hhi = h.astype(BF16)
    hlo = (h - hhi.astype(F32)).astype(BF16)
    whi = whi_ref[...]
    logits = _dot_nt(whi, hhi) + _dot_nt(whi, hlo) + _dot_nt(wlo_ref[...], hhi)
    scores = jax.nn.sigmoid(logits)
    sel = scores + eb_ref[...]
    gsz = n_exp // N_GROUPS
    eio = lax.broadcasted_iota(I32, (n_exp, tm), 0).astype(F32)
    gio = lax.broadcasted_iota(I32, (gsz, tm), 0).astype(F32)
    gs_rows = []
    for g in range(N_GROUPS):
        sg = sel[g * gsz:(g + 1) * gsz]
        m1 = jnp.max(sg, axis=0, keepdims=True)
        i1 = jnp.min(jnp.where(sg == m1, gio, float(gsz)), axis=0, keepdims=True)
        m2 = jnp.max(jnp.where(gio == i1, -jnp.inf, sg), axis=0, keepdims=True)
        gs_rows.append(m1 + m2)
    gs = jnp.concatenate(gs_rows, axis=0)
    grow = lax.broadcasted_iota(I32, (N_GROUPS, tm), 0)
    beaten = jnp.zeros((N_GROUPS, tm), F32)
    for g2 in range(N_GROUPS):
        o = gs_rows[g2]
        beats = jnp.where(o > gs, 1.0, jnp.where((o == gs) & (grow > g2), 1.0, 0.0))
        beaten = beaten + beats
    gkeep = jnp.where(beaten < float(TOPK_GROUPS), 1.0, 0.0)
    selm = jnp.concatenate(
        [jnp.where(gkeep[g:g + 1] > 0.5, sel[g * gsz:(g + 1) * gsz], -jnp.inf) for g in range(N_GROUPS)],
        axis=0)
    member = jnp.zeros((n_exp, tm), F32)
    idxs, gates = [], []
    for _ in range(TOP_K):
        m = jnp.max(selm, axis=0, keepdims=True)
        ik = jnp.min(jnp.where(selm == m, eio, float(n_exp)), axis=0, keepdims=True)
        hit = eio == ik
        gates.append(jnp.sum(jnp.where(hit, scores, 0.0), axis=0, keepdims=True))
        idxs.append(ik)
        selm = jnp.where(hit, -jnp.inf, selm)
        member = jnp.where(hit, 1.0, member)
    gsum = gates[0]
    for gk in gates[1:]:
        gsum = gsum + gk
    tr = lax.broadcasted_iota(I32, (tm, tm), 0)
    tc = lax.broadcasted_iota(I32, (tm, tm), 1)
    before = jnp.where(tr < tc, 1.0, 0.0).astype(BF16)
    prefix = _dot(member.astype(BF16), before) + carry[:, :1]
    ranks = [jnp.sum(jnp.where(eio == ik, prefix, 0.0), axis=0, keepdims=True) for ik in idxs]
    carry[...] = carry[...] + jnp.sum(member, axis=1, keepdims=True)
    idx_ref[...] = jnp.concatenate(idxs, axis=0).astype(I32)
    gate_ref[...] = jnp.concatenate([gk / gsum * ROUTED_SCALE for gk in gates], axis=0)
    rank_ref[...] = jnp.concatenate(ranks, axis=0).astype(I32)
    cnt_ref[...] = carry[...].astype(I32)


def _router(h, whi, wlo, eb, cnt0, tm):
    n, d = h.shape
    n_exp = whi.shape[0]
    tok = pl.BlockSpec((TOP_K, tm), lambda i: (0, i))
    full = lambda s: pl.BlockSpec(s, lambda i: (0, 0))
    return pl.pallas_call(
        functools.partial(_router_kernel, n_exp=n_exp, tm=tm),
        grid=(n // tm,),
        in_specs=[pl.BlockSpec((tm, d), lambda i: (i, 0)), full((n_exp, d)), full((n_exp, d)),
                  full((n_exp, 1)), full((n_exp, LANES))],
        out_specs=[tok, tok, tok, full((n_exp, LANES))],
        out_shape=[jax.ShapeDtypeStruct((TOP_K, n), I32), jax.ShapeDtypeStruct((TOP_K, n), F32),
                   jax.ShapeDtypeStruct((TOP_K, n), I32), jax.ShapeDtypeStruct((n_exp, LANES), I32)],
        scratch_shapes=[pltpu.VMEM((n_exp, LANES), F32)],
        compiler_params=_cparams(("arbitrary",)),
    )(h, whi, wlo, eb, cnt0)


def _dest_kernel(idx_ref, rank_ref, ps_ref, o_ref, *, n_exp, tm):
    eio = lax.broadcasted_iota(I32, (n_exp, tm), 0)
    ps = ps_ref[...]
    rows = []
    for k in range(TOP_K):
        hit = eio == idx_ref[k:k + 1, :]
        rows.append(jnp.sum(jnp.where(hit, ps, 0.0), axis=0, keepdims=True))
    o_ref[...] = jnp.concatenate(rows, axis=0).astype(I32) + rank_ref[...]


def _dest(idx, rank, pstart, tm):
    n = idx.shape[1]
    n_exp = pstart.shape[0]
    tok = pl.BlockSpec((TOP_K, tm), lambda i: (0, i))
    return pl.pallas_call(
        functools.partial(_dest_kernel, n_exp=n_exp, tm=tm),
        grid=(n // tm,),
        in_specs=[tok, tok, pl.BlockSpec((n_exp, 1), lambda i: (0, 0))],
        out_specs=tok,
        out_shape=jax.ShapeDtypeStruct((TOP_K, n), I32),
        compiler_params=_cparams(("arbitrary",)),
    )(idx, rank, pstart.astype(F32).reshape(n_exp, 1))


def _zero_pads_kernel(last_ref, has_ref, xs_ref, zbuf, sem):
    e = pl.program_id(0)
    n_exp = pl.num_programs(0)

    def zero_copy(start):
        return pltpu.make_async_copy(zbuf, xs_ref.at[pl.ds(pl.multiple_of(start, EXPERT_ROWS), EXPERT_ROWS)], sem)

    @pl.when(e == 0)
    def _():
        zbuf[...] = jnp.zeros_like(zbuf)

    @pl.when(has_ref[e] > 0)
    def _():
        zero_copy(last_ref[e]).start()

    @pl.when(e == n_exp - 1)
    def _():
        def drain(j, carry):
            @pl.when(has_ref[j] > 0)
            def _():
                zero_copy(0).wait()
            return carry
        lax.fori_loop(0, n_exp, drain, 0)


def _zero_pads(last_start, has_rows, n_rows, half):
    n_exp = last_start.shape[0]
    grid_spec = pltpu.PrefetchScalarGridSpec(
        num_scalar_prefetch=2,
        grid=(n_exp,),
        in_specs=[],
        out_specs=pl.BlockSpec(memory_space=pl.ANY),
        scratch_shapes=[pltpu.VMEM((EXPERT_ROWS, half), U32), pltpu.SemaphoreType.DMA(())],
    )
    return pl.pallas_call(
        _zero_pads_kernel,
        grid_spec=grid_spec,
        out_shape=jax.ShapeDtypeStruct((n_rows, half), U32),
        compiler_params=_cparams(("arbitrary",)),
    )(last_start, has_rows)


def _dispatch_kernel(dest_ref, hp_ref, xs_in_ref, xs_ref, dsm, sem_d, sem, *, tm):
    del xs_in_ref
    i = pl.program_id(0)
    cp = pltpu.make_async_copy(dest_ref.at[i], dsm, sem_d)
    cp.start()
    cp.wait()

    def row_copy(t, d):
        return pltpu.make_async_copy(hp_ref.at[pl.ds(t, 1)], xs_ref.at[pl.ds(d, 1)], sem)

    def issue(t, carry):
        for k in range(TOP_K):
            row_copy(t, dsm[k, t]).start()
        return carry

    lax.fori_loop(0, tm, issue, 0)

    def drain(t, carry):
        for k in range(TOP_K):
            row_copy(0, 0).wait()
        return carry

    lax.fori_loop(0, tm, drain, 0)


def _dispatch(dest_tiles, hp, xs, tm):
    n, half = hp.shape
    return pl.pallas_call(
        functools.partial(_dispatch_kernel, tm=tm),
        grid=(n // tm,),
        in_specs=[pl.BlockSpec(memory_space=pl.ANY),
                  pl.BlockSpec((tm, half), lambda i: (i, 0)),
                  pl.BlockSpec(memory_space=pl.ANY)],
        out_specs=pl.BlockSpec(memory_space=pl.ANY),
        out_shape=jax.ShapeDtypeStruct(xs.shape, U32),
        scratch_shapes=[pltpu.SMEM((TOP_K, tm), I32), pltpu.SemaphoreType.DMA(()), pltpu.SemaphoreType.DMA(())],
        input_output_aliases={2: 0},
        compiler_params=_cparams(("arbitrary",)),
    )(dest_tiles, hp, xs)


def _experts_kernel(be_ref, nu_ref, nxt_ref, x_ref, wg_hbm, wu_hbm, wd_hbm, y_ref,
                    wg_f, wu_f, wd_f, wg_s, wu_s, wd_s, slot_ref, sem, *, half, n_exp):
    b = pl.program_id(0)
    active = b < nu_ref[0]
    e = be_ref[b]
    new_expert = (b == 0) | (e != be_ref[jnp.maximum(b - 1, 0)])

    def weight_copies(expert, s):
        return (pltpu.make_async_copy(wg_hbm.at[expert], wg_f.at[s], sem.at[s]),
                pltpu.make_async_copy(wu_hbm.at[expert], wu_f.at[s], sem.at[s]),
                pltpu.make_async_copy(wd_hbm.at[expert], wd_f.at[s], sem.at[s]))

    @pl.when(active & (b == 0))
    def _():
        slot_ref[0] = 0
        for cp in weight_copies(e, 0):
            cp.start()

    @pl.when(active & new_expert)
    def _():
        s = slot_ref[0]
        for cp in weight_copies(e, s):
            cp.wait()
        nxt = nxt_ref[e]

        @pl.when(nxt < n_exp)
        def _():
            for cp in weight_copies(nxt, 1 - s):
                cp.start()

        wg_s[...] = wg_f[s].astype(BF16)
        wu_s[...] = wu_f[s].astype(BF16)
        wd_s[...] = wd_f[s].astype(BF16)
        slot_ref[0] = 1 - s

    @pl.when(active)
    def _():
        lo, hi = _unpack_pairs(x_ref[...])
        lo = lo.astype(BF16)
        hi = hi.astype(BF16)
        g = _dot(lo, wg_s[:half, :]) + _dot(hi, wg_s[half:, :])
        u = _dot(lo, wu_s[:half, :]) + _dot(hi, wu_s[half:, :])
        hm = (_silu(g) * u).astype(BF16)
        y_ref[...] = _pack_pairs(_dot(hm, wd_s[...]))


def _experts(block_e, n_used, next_expert, xs, wg, wu, wd):
    n_rows, half = xs.shape
    n_exp, d, de = wg.shape
    nb = n_rows // EXPERT_ROWS
    blk = lambda b, be, nu, nx: (jnp.minimum(b, nu[0] - 1), 0)
    hbm = pl.BlockSpec(memory_space=pl.ANY)
    grid_spec = pltpu.PrefetchScalarGridSpec(
        num_scalar_prefetch=3,
        grid=(nb,),
        in_specs=[pl.BlockSpec((EXPERT_ROWS, half), blk), hbm, hbm, hbm],
        out_specs=pl.BlockSpec((EXPERT_ROWS, half), blk),
        scratch_shapes=[pltpu.VMEM((2, d, de), F32), pltpu.VMEM((2, d, de), F32), pltpu.VMEM((2, de, d), F32),
                        pltpu.VMEM((d, de), BF16), pltpu.VMEM((d, de), BF16), pltpu.VMEM((de, d), BF16),
                        pltpu.SMEM((1,), I32), pltpu.SemaphoreType.DMA((2,))],
    )
    return pl.pallas_call(
        functools.partial(_experts_kernel, half=half, n_exp=n_exp),
        grid_spec=grid_spec,
        out_shape=jax.ShapeDtypeStruct((n_rows, half), U32),
        compiler_params=_cparams(("arbitrary",)),
    )(block_e, n_used, next_expert, xs, wg, wu, wd)


def _combine_kernel(dest_ref, h_ref, gate_ref, ys_ref, wsg_ref, wsu_ref, wsd_ref, g_ref, b_ref, y_ref,
                    buf, dsm, sem_d, sem, *, tm, alpha):
    i = pl.program_id(0)
    last = pl.num_programs(0) - 1
    slot = i % 2
    nslot = 1 - slot

    def table_copy(tile, s):
        return pltpu.make_async_copy(dest_ref.at[jnp.minimum(tile, last)], dsm.at[s], sem_d.at[s])

    def row_copy(s, k, t, d):
        return pltpu.make_async_copy(ys_ref.at[pl.ds(d, 1)], buf.at[s, k, pl.ds(t, 1)], sem.at[s])

    def wait_rows(s):
        def drain(t, carry):
            for k in range(TOP_K):
                row_copy(s, 0, 0, 0).wait()
            return carry
        lax.fori_loop(0, tm, drain, 0)

    @pl.when(i == 0)
    def _():
        table_copy(0, 0).start()
        table_copy(0, 0).wait()

        def issue(t, carry):
            for k in range(TOP_K):
                row_copy(0, k, t, dsm[0, k, t]).start()
            return carry
        lax.fori_loop(0, tm, issue, 0)
        table_copy(1, 1).start()

    table_copy(i + 1, nslot).wait()
    wait_rows(slot)
    table_copy(i + 2, slot).start()

    for t in range(tm):
        for k in range(TOP_K):
            row_copy(nslot, k, t, dsm[nslot, k, t]).start()

    h = h_ref[...]
    hb = h.astype(BF16)
    sh = _dot((_silu(_dot(hb, wsg_ref[...])) * _dot(hb, wsu_ref[...])).astype(BF16), wsd_ref[...])

    gate = gate_ref[...]
    acc_lo = jnp.zeros((tm, buf.shape[3]), F32)
    acc_hi = jnp.zeros((tm, buf.shape[3]), F32)
    for k in range(TOP_K):
        lo, hi = _unpack_pairs(buf[slot, k])
        gk = gate[:, k:k + 1]
        acc_lo = acc_lo + gk * lo
        acc_hi = acc_hi + gk * hi
    routed = jnp.concatenate([acc_lo, acc_hi], axis=1)
    y_ref[...] = _layer_norm(alpha * h + (routed + sh), g_ref[...], b_ref[...])

    @pl.when(i == last)
    def _():
        wait_rows(nslot)
        table_copy(i + 2, slot).wait()


def _combine(dest_tiles, h, gate_t, ys, wsg, wsu, wsd, g, b, alpha, tm):
    n, d = h.shape
    half = ys.shape[1]
    full = lambda a: pl.BlockSpec(a.shape, lambda i: (0, 0))
    return pl.pallas_call(
        functools.partial(_combine_kernel, tm=tm, alpha=alpha),
        grid=(n // tm,),
        in_specs=[pl.BlockSpec(memory_space=pl.ANY),
                  pl.BlockSpec((tm, d), lambda i: (i, 0)),
                  pl.BlockSpec((tm, TOP_K), lambda i: (i, 0)),
                  pl.BlockSpec(memory_space=pl.ANY),
                  full(wsg), full(wsu), full(wsd), full(g), full(b)],
        out_specs=pl.BlockSpec((tm, d), lambda i: (i, 0)),
        out_shape=jax.ShapeDtypeStruct((n, d), F32),
        scratch_shapes=[pltpu.VMEM((2, TOP_K, tm, half), U32), pltpu.SMEM((2, TOP_K, tm), I32),
                        pltpu.SemaphoreType.DMA((2,)), pltpu.SemaphoreType.DMA((2,))],
        compiler_params=_cparams(("arbitrary",)),
    )(dest_tiles, h, gate_t, ys, wsg, wsu, wsd, g, b)


def _rope_tables(pos):
    half = HEAD_DIM // 2
    inv_freq = ROPE_BASE ** (-jnp.arange(half, dtype=F32) / half)
    ang = pos.astype(F32)[:, None] * inv_freq[None, :]
    cos, sin = jnp.cos(ang), jnp.sin(ang)
    return jnp.concatenate([cos, cos], axis=-1), jnp.concatenate([-sin, sin], axis=-1)


def _dest_tiles(dest, tm):
    k, n = dest.shape
    return dest.reshape(k, n // tm, tm).transpose(1, 0, 2)


def _layer(xp, xs, state_ret, cache_k, cache_v, cache_logf, w_in, b_fgate, ret_gn_g, w_out,
           ln1_g, ln1_b, w_router, e_bias, w_e_gate, w_e_up, w_e_down, w_s_gate, w_s_up, w_s_down,
           ln2_g, ln2_b, alpha):
    bp, tp, d = xp.shape
    bs, ts, _ = xs.shape
    past = cache_k.shape[1]
    width = w_out.shape[0] // 2
    n_heads = width // HEAD_DIM
    n_exp = w_router.shape[1]

    w_ret = w_in[:, :4 * width].astype(BF16)
    w_fox = w_in[:, 4 * width:7 * width].astype(BF16)
    n_f = w_in.shape[1] - 7 * width
    w_f = jnp.pad(w_in[:, 7 * width:], ((0, 0), (0, LANES - n_f))).astype(BF16)
    b_f = jnp.pad(b_fgate, (0, LANES - n_f)).reshape(1, LANES)
    gn = ret_gn_g.reshape(1, width)
    w_o = w_out.astype(BF16)
    l1g, l1b = ln1_g.reshape(1, d), ln1_b.reshape(1, d)
    l2g, l2b = ln2_g.reshape(1, d), ln2_b.reshape(1, d)
    wr_t = w_router.T
    wr_top = _truncate_to_bf16(wr_t)
    wr_hi = wr_top.astype(BF16)
    wr_lo = (wr_t - wr_top).astype(BF16)
    eb = e_bias.reshape(n_exp, 1)
    wsg, wsu, wsd = w_s_gate.astype(BF16), w_s_up.astype(BF16), w_s_down.astype(BF16)

    xp2 = xp.reshape(bp * tp, d)
    tm_p = _tile(tp, 512)
    cs_p, sn_p = _rope_tables(jnp.arange(tp))
    pr = _proj_ret(xp2, w_ret, cs_p, sn_p, tp, tm_p).reshape(bp, tp, 4 * width)
    fk, fv, lf, qa_p, ka_p, vt_p = _proj_fox_heads(xp2, w_fox, w_f, b_f, bp, _tile(tp, 256))
    s0 = jnp.zeros((bp, n_heads, HEAD_DIM, HEAD_DIM), F32)
    ry_p, sfin_p = _retention(pr, s0, gn, _tile(tp, 256))
    lf_p = lf.reshape(bp, tp, LANES)
    tq_p = _tile(tp, 2048)
    fo_p = _fox(qa_p, ka_p, vt_p, 0, tq_p, tq_p // 2, n_chains=2)
    h_p, hp_p = _finish(xp2, ry_p.reshape(bp * tp, width), fo_p.reshape(bp * tp, width), w_o, l1g, l1b,
                        alpha, _tile(bp * tp, 256))

    xs2 = xs.reshape(bs * ts, d)
    tm_s = _tile(ts, 512)
    cs_s, sn_s = _rope_tables(past + jnp.arange(ts))
    prs = _proj_ret(xs2, w_ret, cs_s, sn_s, ts, tm_s).reshape(bs, ts, 4 * width)
    fq_s, fk_s, fv_s, fkb_s, fvb_s, lf_sn = _proj_fox(xs2, w_fox, w_f, b_f, tm_s)
    ry_s, sfin_s = _retention(prs, state_ret.astype(F32), gn, ts)
    tk_s = LANES
    tq_s = -(-ts // LANES) * LANES
    t_all = -(-(past + tq_s) // tk_s) * tk_s
    pad_t = t_all - past - ts
    lf_s3 = lf_sn.reshape(bs, ts, LANES)
    lf_all = jnp.concatenate([
        jnp.pad(cache_logf.astype(F32), ((0, 0), (0, 0), (0, LANES - n_heads))),
        lf_s3, jnp.zeros((bs, pad_t, LANES), F32)], axis=1)
    c_s = _cumsum(lf_all, tk_s)
    zpad = jnp.zeros((bs, pad_t, width), BF16)
    k_all = jnp.concatenate([cache_k.reshape(bs, past, width).astype(BF16),
                             fkb_s.reshape(bs, ts, width), zpad], axis=1)
    v_all = jnp.concatenate([cache_v.reshape(bs, past, width).astype(BF16),
                             fvb_s.reshape(bs, ts, width), zpad], axis=1)
    qa_s, ka_s, vt_s = _fox_operands(fq_s.reshape(bs, ts, width), k_all, v_all, c_s, tq_s, past)
    fo_s = _fox(qa_s, ka_s, vt_s, past, tq_s, t_all)[:, :ts]
    h_s, hp_s = _finish(xs2, ry_s.reshape(bs * ts, width), fo_s.reshape(bs * ts, width), w_o, l1g, l1b,
                        alpha, _tile(bs * ts, 256))

    n_p, n_s = bp * tp, bs * ts
    tm_r = 256
    cnt0 = jnp.zeros((n_exp, LANES), I32)
    idx_p, gate_p, rank_p, cnt1 = _router(h_p, wr_hi, wr_lo, eb, cnt0, _tile(n_p, tm_r))
    idx_s, gate_s, rank_s, cnt2 = _router(h_s, wr_hi, wr_lo, eb, cnt1, _tile(n_s, tm_r))
    counts = cnt2[:, 0]
    padded = (counts + EXPERT_ROWS - 1) // EXPERT_ROWS * EXPERT_ROWS
    pend = jnp.cumsum(padded)
    pstart = pend - padded
    dest_p = _dest(idx_p, rank_p, pstart, _tile(n_p, 512))
    dest_s = _dest(idx_s, rank_s, pstart, _tile(n_s, 512))
    n_blocks = -(-((n_p + n_s) * TOP_K) // EXPERT_ROWS) + n_exp
    n_used = (pend[-1] // EXPERT_ROWS).astype(I32).reshape(1)
    block_start = jnp.arange(n_blocks, dtype=I32) * EXPERT_ROWS
    block_e = jnp.minimum(jnp.sum((pend[None, :] <= block_start[:, None]).astype(I32), axis=1),
                          n_exp - 1).astype(I32)
    tm_d = 256
    dt_p = _dest_tiles(dest_p, _tile(n_p, tm_d))
    dt_s = _dest_tiles(dest_s, _tile(n_s, tm_d))
    xs_rows = _zero_pads((pend - EXPERT_ROWS).astype(I32), padded.astype(I32), n_blocks * EXPERT_ROWS, d // 2)
    xs_rows = _dispatch(dt_p, hp_p, xs_rows, _tile(n_p, tm_d))
    xs_rows = _dispatch(dt_s, hp_s, xs_rows, _tile(n_s, tm_d))
    owners = jnp.where(counts > 0, jnp.arange(n_exp, dtype=I32), n_exp)
    later = lax.cummin(owners[::-1])[::-1]
    next_expert = jnp.concatenate([later[1:], jnp.full((1,), n_exp, I32)]).astype(I32)
    ys_rows = _experts(block_e, n_used, next_expert, xs_rows, w_e_gate, w_e_up, w_e_down)
    tm_c = 128
    y_p = _combine(_dest_tiles(dest_p, _tile(n_p, tm_c)), h_p, gate_p.T, ys_rows, wsg, wsu, wsd, l2g, l2b,
                   alpha, _tile(n_p, tm_c))
    y_s = _combine(_dest_tiles(dest_s, _tile(n_s, tm_c)), h_s, gate_s.T, ys_rows, wsg, wsu, wsd, l2g, l2b,
                   alpha, _tile(n_s, tm_c))

    outs_p = (sfin_p, fk.reshape(bp, tp, n_heads, HEAD_DIM), fv.reshape(bp, tp, n_heads, HEAD_DIM),
              lf_p[:, :, :n_heads])
    outs_s = (sfin_s, fk_s.reshape(bs, ts, n_heads, HEAD_DIM), fv_s.reshape(bs, ts, n_heads, HEAD_DIM),
              lf_s3[:, :, :n_heads])
    return y_p.reshape(bp, tp, d), y_s.reshape(bs, ts, d), outs_p, outs_s


def kernel(x_prompt, x_sample, state_ret, cache_fox_k, cache_fox_v, cache_fox_logf, w_in, b_fgate, ret_gn_g, w_out, ln1_g, ln1_b, w_router, e_bias, w_e_gate, w_e_up, w_e_down, w_s_gate, w_s_up, w_s_down, ln2_g, ln2_b):
    depth = w_in.shape[0]
    alpha = (2.0 * depth) ** 0.25
    xp, xs = x_prompt, x_sample
    per_p, per_s = [], []
    for l in range(depth):
        xp, xs, op, os_ = _layer(
            xp, xs, state_ret[l], cache_fox_k[l], cache_fox_v[l], cache_fox_logf[l], w_in[l], b_fgate[l],
            ret_gn_g[l], w_out[l], ln1_g[l], ln1_b[l], w_router[l], e_bias[l], w_e_gate[l], w_e_up[l],
            w_e_down[l], w_s_gate[l], w_s_up[l], w_s_down[l], ln2_g[l], ln2_b[l], alpha)
        per_p.append(op)
        per_s.append(os_)
    stack = lambda items, j: jnp.stack([it[j] for it in items])
    return (xp, xs,
            stack(per_p, 0), stack(per_p, 1), stack(per_p, 2), stack(per_p, 3),
            stack(per_s, 0).astype(state_ret.dtype), stack(per_s, 1), stack(per_s, 2),
            stack(per_s, 3).astype(cache_fox_logf.dtype))
```

```python
import functools
import math

import jax
import jax.numpy as jnp
from jax import lax
from jax.experimental import pallas as pl
from jax.experimental.pallas import tpu as pltpu

HEAD_DIM = 128
ROPE_BASE = 10000.0
N_GROUPS = 8
TOPK_GROUPS = 4
TOP_K = 8
ROUTED_SCALE = 2.5
LN_EPS = 1e-5
LANES = 128
EXPERT_ROWS = 256
LOG2_E = math.log2(math.e)
VMEM_LIMIT_BYTES = 56 * 1024 * 1024

F32 = jnp.float32
BF16 = jnp.bfloat16
U32 = jnp.uint32
I32 = jnp.int32


def _cparams(sem):
    return pltpu.CompilerParams(dimension_semantics=sem, vmem_limit_bytes=VMEM_LIMIT_BYTES)


def _tile(n, pref):
    t = min(n, pref)
    while n % t:
        t -= 1
    return t


def _dot(a, b):
    return jnp.dot(a, b, preferred_element_type=F32)


def _dot_nt(a, b):
    return lax.dot_general(a, b, (((1,), (1,)), ((), ())), preferred_element_type=F32)


def _dot_tn(a, b):
    return lax.dot_general(a, b, (((0,), (0,)), ((), ())), preferred_element_type=F32)


def _pack_pairs(a):
    n = a.shape[1] // 2
    lo = lax.bitcast_convert_type(a[:, :n].astype(BF16).astype(F32), U32)
    hi = lax.bitcast_convert_type(a[:, n:].astype(BF16).astype(F32), U32)
    return (lo >> 16) | (hi & jnp.uint32(0xFFFF0000))


def _unpack_pairs(u):
    lo = lax.bitcast_convert_type(u << 16, F32)
    hi = lax.bitcast_convert_type(u & jnp.uint32(0xFFFF0000), F32)
    return lo, hi


def _layer_norm(z, g, b):
    mu = jnp.mean(z, axis=-1, keepdims=True)
    zc = z - mu
    var = jnp.mean(zc * zc, axis=-1, keepdims=True)
    return zc * lax.rsqrt(var + LN_EPS) * g + b


def _silu(g):
    return g * jax.nn.sigmoid(g)


def _proj_ret_kernel(x_ref, w_ref, cs_ref, sn_ref, o_ref, *, width, scale):
    xb = x_ref[...].astype(BF16)
    cs = cs_ref[...]
    sn = sn_ref[...]
    for sec in range(4):
        p = _dot(xb, w_ref[:, sec * width:(sec + 1) * width])
        if sec < 2:
            for h in range(width // HEAD_DIM):
                ph = p[:, h * HEAD_DIM:(h + 1) * HEAD_DIM]
                r = ph * cs + pltpu.roll(ph, HEAD_DIM // 2, 1) * sn
                if sec == 1:
                    r = r * scale
                o_ref[:, sec * width + h * HEAD_DIM:sec * width + (h + 1) * HEAD_DIM] = r.astype(BF16)
        else:
            o_ref[:, sec * width:(sec + 1) * width] = p.astype(BF16)


def _proj_ret(x, w, cs, sn, seq, tm):
    n, d = x.shape
    width = w.shape[1] // 4
    nt_seq = seq // tm
    return pl.pallas_call(
        functools.partial(_proj_ret_kernel, width=width, scale=HEAD_DIM ** -0.5),
        grid=(n // tm,),
        in_specs=[
            pl.BlockSpec((tm, d), lambda i: (i, 0)),
            pl.BlockSpec((d, 4 * width), lambda i: (0, 0)),
            pl.BlockSpec((tm, HEAD_DIM), lambda i: (i % nt_seq, 0)),
            pl.BlockSpec((tm, HEAD_DIM), lambda i: (i % nt_seq, 0)),
        ],
        out_specs=pl.BlockSpec((tm, 4 * width), lambda i: (i, 0)),
        out_shape=jax.ShapeDtypeStruct((n, 4 * width), BF16),
        compiler_params=_cparams(("arbitrary",)),
    )(x, w, cs, sn)


def _proj_fox_kernel(x_ref, w_ref, wf_ref, bf_ref, q_ref, k_ref, v_ref, kb_ref, vb_ref, lf_ref,
                     *, width, scale):
    xb = x_ref[...].astype(BF16)
    q = _dot(xb, w_ref[:, :width])
    q_ref[...] = (q * scale).astype(BF16)
    k = _dot(xb, w_ref[:, width:2 * width])
    k_ref[...] = k
    kb_ref[...] = k.astype(BF16)
    v = _dot(xb, w_ref[:, 2 * width:])
    v_ref[...] = v
    vb_ref[...] = v.astype(BF16)
    z = _dot(xb, wf_ref[...]) + bf_ref[...]
    lf_ref[...] = jnp.minimum(z, 0.0) - jnp.log1p(jnp.exp(-jnp.abs(z)))


def _log_sigmoid(z):
    return jnp.minimum(z, 0.0) - jnp.log1p(jnp.exp(-jnp.abs(z)))


def _proj_fox_heads_kernel(x_ref, w_ref, wf_ref, bf_ref, k_ref, v_ref, lf_ref, c_ref, qa_ref, ka_ref, vt_ref,
                           carry, *, width, scale, n_heads):
    @pl.when(pl.program_id(1) == 0)
    def _():
        carry[...] = jnp.zeros_like(carry)

    tm = x_ref.shape[0]
    xb = x_ref[...].astype(BF16)
    q = _dot(xb, w_ref[:, :width]) * scale
    k = _dot(xb, w_ref[:, width:2 * width])
    v = _dot(xb, w_ref[:, 2 * width:])
    k_ref[...] = k
    v_ref[...] = v
    logf = _log_sigmoid(_dot(xb, wf_ref[...]) + bf_ref[...])
    lf_ref[...] = logf
    c = _tile_cumsum(logf, carry[...])
    c_ref[...] = c
    carry[...] = c[tm - 1:tm, :]
    c2 = c * LOG2_E
    lane = lax.broadcasted_iota(I32, (tm, HEAD_DIM), 1)
    for h in range(n_heads):
        sl = slice(h * HEAD_DIM, (h + 1) * HEAD_DIM)
        hi, mid, lo = _split3(c2[:, h:h + 1])
        terms = jnp.where(lane == 0, hi.astype(F32), jnp.where(lane == 1, mid.astype(F32), lo.astype(F32)))
        q_tail = jnp.where(lane < 3, terms, jnp.where(lane < 6, 1.0, 0.0))
        k_tail = jnp.where(lane < 3, 1.0, jnp.where(lane < 6, -pltpu.roll(terms, 3, 1), 0.0))
        qa_ref[0, h, :, :HEAD_DIM] = q[:, sl].astype(BF16)
        qa_ref[0, h, :, HEAD_DIM:] = q_tail.astype(BF16)
        ka_ref[0, h, :, :HEAD_DIM] = k[:, sl].astype(BF16)
        ka_ref[0, h, :, HEAD_DIM:] = k_tail.astype(BF16)
        vt_ref[0, h] = v[:, sl].T.astype(BF16)


def _proj_fox_heads(x, w, wf, bfg, batch, tm):
    n, d = x.shape
    width = w.shape[1] // 3
    n_heads = width // HEAD_DIM
    seq = n // batch
    nt = seq // tm
    row = lambda c: pl.BlockSpec((tm, c), lambda b, i: (b * nt + i, 0))
    const = lambda s: pl.BlockSpec(s, lambda b, i: (0, 0))
    aug = pl.BlockSpec((1, n_heads, tm, 2 * HEAD_DIM), lambda b, i: (b, 0, i, 0))
    return pl.pallas_call(
        functools.partial(_proj_fox_heads_kernel, width=width, scale=HEAD_DIM ** -0.5 * LOG2_E,
                          n_heads=n_heads),
        grid=(batch, nt),
        in_specs=[row(d), const((d, 3 * width)), const((d, LANES)), const((1, LANES))],
        out_specs=[row(width), row(width), row(LANES), row(LANES), aug, aug,
                   pl.BlockSpec((1, n_heads, HEAD_DIM, tm), lambda b, i: (b, 0, 0, i))],
        out_shape=[
            jax.ShapeDtypeStruct((n, width), F32),
            jax.ShapeDtypeStruct((n, width), F32),
            jax.ShapeDtypeStruct((n, LANES), F32),
            jax.ShapeDtypeStruct((n, LANES), F32),
            jax.ShapeDtypeStruct((batch, n_heads, seq, 2 * HEAD_DIM), BF16),
            jax.ShapeDtypeStruct((batch, n_heads, seq, 2 * HEAD_DIM), BF16),
            jax.ShapeDtypeStruct((batch, n_heads, HEAD_DIM, seq), BF16),
        ],
        scratch_shapes=[pltpu.VMEM((1, LANES), F32)],
        compiler_params=_cparams(("arbitrary", "arbitrary")),
    )(x, w, wf, bfg)


def _proj_fox(x, w, wf, bfg, tm):
    n, d = x.shape
    width = w.shape[1] // 3
    row = lambda c: pl.BlockSpec((tm, c), lambda i: (i, 0))
    return pl.pallas_call(
        functools.partial(_proj_fox_kernel, width=width, scale=HEAD_DIM ** -0.5 * LOG2_E),
        grid=(n // tm,),
        in_specs=[
            row(d),
            pl.BlockSpec((d, 3 * width), lambda i: (0, 0)),
            pl.BlockSpec((d, LANES), lambda i: (0, 0)),
            pl.BlockSpec((1, LANES), lambda i: (0, 0)),
        ],
        out_specs=[row(width), row(width), row(width), row(width), row(width), row(LANES)],
        out_shape=[
            jax.ShapeDtypeStruct((n, width), BF16),
            jax.ShapeDtypeStruct((n, width), F32),
            jax.ShapeDtypeStruct((n, width), F32),
            jax.ShapeDtypeStruct((n, width), BF16),
            jax.ShapeDtypeStruct((n, width), BF16),
            jax.ShapeDtypeStruct((n, LANES), F32),
        ],
        compiler_params=_cparams(("arbitrary",)),
    )(x, w, wf, bfg)


def _split3(x):
    hi = x.astype(BF16)
    r1 = x - hi.astype(F32)
    mid = r1.astype(BF16)
    lo = (r1 - mid.astype(F32)).astype(BF16)
    return hi, mid, lo


def _tile_cumsum(x, carry_row):
    tm = x.shape[0]
    r = lax.broadcasted_iota(I32, (tm, tm), 0)
    c = lax.broadcasted_iota(I32, (tm, tm), 1)
    tri = jnp.where(c <= r, 1.0, 0.0).astype(BF16)
    hi, mid, lo = _split3(x)
    return _dot(tri, hi) + _dot(tri, mid) + _dot(tri, lo) + carry_row


def _cumsum_kernel(x_ref, o_ref, carry, *, tm):
    @pl.when(pl.program_id(1) == 0)
    def _():
        carry[...] = jnp.zeros_like(carry)

    out = _tile_cumsum(x_ref[0], carry[...])
    o_ref[0] = out
    carry[...] = out[tm - 1:tm, :]


def _cumsum(x, tm):
    b, t, _ = x.shape
    return pl.pallas_call(
        functools.partial(_cumsum_kernel, tm=tm),
        grid=(b, t // tm),
        in_specs=[pl.BlockSpec((1, tm, LANES), lambda i, j: (i, j, 0))],
        out_specs=pl.BlockSpec((1, tm, LANES), lambda i, j: (i, j, 0)),
        out_shape=jax.ShapeDtypeStruct(x.shape, F32),
        scratch_shapes=[pltpu.VMEM((1, LANES), F32)],
        compiler_params=_cparams(("arbitrary", "arbitrary")),
    )(x)


def _retention_kernel(q_ref, k_ref, v_ref, g_ref, s0_ref, gn_ref, y_ref, sout_ref, s_scr,
                      *, n_heads, chunk):
    c = pl.program_id(1)

    @pl.when(c == 0)
    def _():
        s_scr[...] = s0_ref[0]

    row = lax.broadcasted_iota(I32, (chunk, chunk), 0)
    col = lax.broadcasted_iota(I32, (chunk, chunk), 1)
    rel = (row - col).astype(F32)
    ri = lax.broadcasted_iota(I32, (chunk, HEAD_DIM), 0).astype(F32)
    for h in range(n_heads):
        sl = slice(h * HEAD_DIM, (h + 1) * HEAD_DIM)
        lg = math.log1p(-(2.0 ** (-5 - h)))
        decay = jnp.where(rel >= 0, jnp.exp(lg * jnp.maximum(rel, 0.0)), 0.0)
        q = q_ref[0, :, sl]
        k = k_ref[0, :, sl]
        v = v_ref[0, :, sl]
        state = s_scr[h]
        scores = _dot_nt(q, k) * decay
        o = _dot(scores.astype(BF16), v)
        o = o + jnp.exp(lg * (ri + 1.0)) * _dot(q, state.astype(BF16))
        kd = (k.astype(F32) * jnp.exp(lg * (chunk - 1.0 - ri))).astype(BF16)
        s_scr[h] = math.exp(lg * chunk) * state + _dot_tn(kd, v)
        mu = jnp.mean(o, axis=-1, keepdims=True)
        oc = o - mu
        var = jnp.mean(oc * oc, axis=-1, keepdims=True)
        yn = oc * lax.rsqrt(var + LN_EPS) * gn_ref[:, sl]
        y_ref[0, :, sl] = (_silu(g_ref[0, :, sl].astype(F32)) * yn).astype(BF16)

    @pl.when(c == pl.num_programs(1) - 1)
    def _():
        sout_ref[0] = s_scr[...]


def _retention(p, s0, gn, chunk):
    b, t, w4 = p.shape
    width = w4 // 4
    n_heads = width // HEAD_DIM
    sec = lambda s: pl.BlockSpec((1, chunk, width), lambda i, j: (i, j, s))
    st = pl.BlockSpec((1, n_heads, HEAD_DIM, HEAD_DIM), lambda i, j: (i, 0, 0, 0))
    return pl.pallas_call(
        functools.partial(_retention_kernel, n_heads=n_heads, chunk=chunk),
        grid=(b, t // chunk),
        in_specs=[sec(0), sec(1), sec(2), sec(3), st, pl.BlockSpec((1, width), lambda i, j: (0, 0))],
        out_specs=[pl.BlockSpec((1, chunk, width), lambda i, j: (i, j, 0)), st],
        out_shape=[jax.ShapeDtypeStruct((b, t, width), BF16),
                   jax.ShapeDtypeStruct(s0.shape, F32)],
        scratch_shapes=[pltpu.VMEM((n_heads, HEAD_DIM, HEAD_DIM), F32)],
        compiler_params=_cparams(("arbitrary", "arbitrary")),
    )(p, p, p, p, s0, gn)


def _fox_kernel(skip_ref, q_ref, k_ref, vt_ref, o_ref, *, tq, tk, q_off, n_kblocks, n_chains):
    qi = pl.program_id(2)
    tqc = tq // n_chains

    def absorb(s, q_first, vt, k0, carry, masked):
        m, l, acc = carry
        if masked:
            kpos = k0 + lax.broadcasted_iota(I32, (tk, tqc), 0)
            qpos = q_first + lax.broadcasted_iota(I32, (tk, tqc), 1)
            s = jnp.where(kpos <= qpos, s, -jnp.inf)
        m_new = jnp.maximum(m, jnp.max(s, axis=0, keepdims=True))
        alpha = jnp.exp2(m - m_new)
        p = jnp.exp2(s - m_new)
        l = alpha * l + jnp.sum(p, axis=0, keepdims=True)
        acc = alpha * acc + _dot(vt, p.astype(BF16))
        return m_new, l, acc

    def kv_block(j):
        k0 = pl.multiple_of(j * tk, tk)
        return k_ref[0, 0, pl.ds(k0, tk), :], vt_ref[0, 0, :, pl.ds(k0, tk)], k0

    init =(jnp.full((1, tqc), -jnp.inf, F32), jnp.zeros((1, tqc), F32), jnp.zeros((HEAD_DIM, tqc), F32))

    if n_chains == 1:
        q = q_ref[0, 0]
        q_first = q_off + qi * tq
        n_full = jnp.minimum((q_first + 1) // tk, n_kblocks)
        n_tot = jnp.minimum((q_first + tq + tk - 1) // tk, n_kblocks)

        def step(j, carry, masked):
            k, vt, k0 = kv_block(j)
            return absorb(_dot_nt(k, q), q_first, vt, k0, carry, masked)

        carry = lax.fori_loop(0, n_full, functools.partial(step, masked=False), init)
        _, l, acc = lax.fori_loop(n_full, n_tot, functools.partial(step, masked=True), carry)
        o_ref[0] = (acc / l).T.astype(BF16)
    else:
        qs = [q_ref[0, 0, c * tqc:(c + 1) * tqc, :] for c in range(n_chains)]
        firsts = [qi * tq + c * tqc for c in range(n_chains)]

        def step(j, carries):
            k, vt, k0 = kv_block(j)
            scores = [_dot_nt(k, qs[c]) for c in range(n_chains)]
            return tuple(absorb(scores[c], firsts[c], vt, k0, carries[c], False) for c in range(n_chains))

        first = skip_ref[(pl.program_id(0) * pl.num_programs(1) + pl.program_id(1)) * pl.num_programs(2) + qi]
        carries = list(lax.fori_loop(first, qi * n_chains, step, (init,) * n_chains))
        for jj in range(n_chains):
            k, vt, k0 = kv_block(qi * n_chains + jj)
            scores = {c: _dot_nt(k, qs[c]) for c in range(jj, n_chains)}
            for c in range(jj, n_chains):
                carries[c] = absorb(scores[c], firsts[c], vt, k0, carries[c], c == jj)
        for c in range(n_chains):
            _, l, acc = carries[c]
            o_ref[0, c * tqc:(c + 1) * tqc, :] = (acc / l).T.astype(BF16)


def _fox(qa, ka, vt, q_off, tq, tk, n_chains=1, skip=None):
    b, n_heads, t_q, da = qa.shape
    t_k = ka.shape[2]
    assert n_chains == 1 or (q_off == 0 and tq == n_chains * tk and t_q == t_k)
    if skip is None:
        skip = jnp.zeros((b * n_heads * (t_q // tq),), I32)
    grid_spec = pltpu.PrefetchScalarGridSpec(
        num_scalar_prefetch=1,
        grid=(b, n_heads, t_q // tq),
        in_specs=[
            pl.BlockSpec((1, 1, tq, da), lambda i, h, j, sk: (i, h, j, 0)),
            pl.BlockSpec((1, 1, t_k, da), lambda i, h, j, sk: (i, h, 0, 0)),
            pl.BlockSpec((1, 1, HEAD_DIM, t_k), lambda i, h, j, sk: (i, h, 0, 0)),
        ],
        out_specs=pl.BlockSpec((1, tq, HEAD_DIM), lambda i, h, j, sk: (i, j, h)),
    )
    return pl.pallas_call(
        functools.partial(_fox_kernel, tq=tq, tk=tk, q_off=q_off, n_kblocks=t_k // tk, n_chains=n_chains),
        grid_spec=grid_spec,
        out_shape=jax.ShapeDtypeStruct((b, t_q, n_heads * HEAD_DIM), BF16),
        compiler_params=_cparams(("arbitrary", "arbitrary", "arbitrary")),
    )(skip, qa, ka, vt)


UNDERFLOW_LOG2 = 160.0


def _fox_skip_table(qa, ka, c, tq, tk):
    b, n_heads, t, _ = qa.shape
    norm = lambda a: jnp.sqrt(jnp.max(jnp.sum(jnp.square(a[..., :HEAD_DIM].astype(F32)), axis=-1), axis=-1))
    bound = norm(qa) * norm(ka)
    c2 = (c[:, :, :n_heads] * LOG2_E).transpose(0, 2, 1)
    c_end = c2[:, :, tk - 1::tk]
    c_q0 = c2[:, :, ::tq]
    gap = 2.0 * bound[:, :, None, None] + c_q0[:, :, :, None] - c_end[:, :, None, :]
    before = (jnp.arange(t // tk)[None, :] < (jnp.arange(t // tq) * (tq // tk))[:, None])
    dead = (gap < -UNDERFLOW_LOG2) & before[None, None]
    return jnp.sum(jnp.cumprod(dead.astype(I32), axis=-1), axis=-1).astype(I32).reshape(-1)


def _truncate_to_bf16(x):
    bits = lax.bitcast_convert_type(x, U32) & jnp.uint32(0xFFFF0000)
    return lax.bitcast_convert_type(bits, F32)


def _fox_operands(fq, fkb, fvb, c, q_rows, q_off):
    b, t_k, width = fkb.shape
    n_heads = width // HEAD_DIM
    t_q = fq.shape[1]
    c2 = (c[:, :, :n_heads] * LOG2_E).transpose(0, 2, 1)
    hi = _truncate_to_bf16(c2)
    r1 = c2 - hi
    mid = _truncate_to_bf16(r1)
    terms = jnp.stack([hi, mid, r1 - mid], axis=-1).astype(BF16)
    ones = jnp.ones_like(terms)
    fill = jnp.zeros((b, n_heads, t_k, HEAD_DIM - 6), BF16)
    heads = lambda a: a.reshape(b, a.shape[1], n_heads, HEAD_DIM).transpose(0, 2, 1, 3)
    ka = jnp.concatenate([heads(fkb), ones, -terms, fill], axis=-1)
    qh = jnp.pad(heads(fq), ((0, 0), (0, 0), (0, q_rows - t_q), (0, 0)))
    qa = jnp.concatenate([qh, terms[:, :, q_off:q_off + q_rows], ones[:, :, :q_rows], fill[:, :, :q_rows]],
                         axis=-1)
    vt = fvb.reshape(b, t_k, n_heads, HEAD_DIM).transpose(0, 2, 3, 1)
    return qa, ka, vt


def _store_row_tiled(ref, a):
    m, w = a.shape
    s = w // LANES
    for j in range(s):
        ref[pl.ds(j, m, stride=s), :] = a[:, j * LANES:(j + 1) * LANES]


def _load_row_tiled(ref, start, m, s):
    return jnp.concatenate([ref[pl.ds(start + j, m, stride=s), :] for j in range(s)], axis=1)


def _finish_kernel(x_ref, ry_ref, fo_ref, w_ref, g_ref, b_ref, h_ref, hp_ref, *, alpha, half):
    mix = _dot(ry_ref[...], w_ref[:half, :]) + _dot(fo_ref[...], w_ref[half:, :])
    h = _layer_norm(alpha * x_ref[...] + mix, g_ref[...], b_ref[...])
    h_ref[...] = h
    _store_row_tiled(hp_ref, _pack_pairs(h))


def _finish(x, ry, fo, w, g, b, alpha, tm):
    n, d = x.shape
    half = ry.shape[1]
    s = d // 2 // LANES
    row = lambda c: pl.BlockSpec((tm, c), lambda i: (i, 0))
    vec = pl.BlockSpec((1, d), lambda i: (0, 0))
    return pl.pallas_call(
        functools.partial(_finish_kernel, alpha=alpha, half=half),
        grid=(n // tm,),
        in_specs=[row(d), row(half), row(fo.shape[1]), pl.BlockSpec(w.shape, lambda i: (0, 0)), vec, vec],
        out_specs=[row(d), pl.BlockSpec((tm * s, LANES), lambda i: (i, 0))],
        out_shape=[jax.ShapeDtypeStruct((n, d), F32), jax.ShapeDtypeStruct((n * s, LANES), U32)],
        compiler_params=_cparams(("arbitrary",)),
    )(x, ry, fo, w, g, b)


def _router_kernel(h_ref, whi_ref, wlo_ref, eb_ref, c0_ref, idx_ref, gate_ref, rank_ref, cnt_ref, carry,
                   *, n_exp, tm):
    @pl.when(pl.program_id(0) == 0)
    def _():
        carry[...] = c0_ref[...].astype(F32)

    h = h_ref[...]
    hhi = h.astype(BF16)
    hlo = (h - hhi.astype(F32)).astype(BF16)
    whi = whi_ref[...]
    logits = _dot_nt(whi, hhi) + _dot_nt(whi, hlo) + _dot_nt(wlo_ref[...], hhi)
    scores = jax.nn.sigmoid(logits)
    sel = scores + eb_ref[...]
    gsz = n_exp // N_GROUPS
    eio = lax.broadcasted_iota(I32, (n_exp, tm), 0).astype(F32)
    gio = lax.broadcasted_iota(I32, (gsz, tm), 0).astype(F32)
    gs_rows = []
    for g in range(N_GROUPS):
        sg = sel[g * gsz:(g + 1) * gsz]
        m1 = jnp.max(sg, axis=0, keepdims=True)
        i1 = jnp.min(jnp.where(sg == m1, gio, float(gsz)), axis=0, keepdims=True)
        m2 = jnp.max(jnp.where(gio == i1, -jnp.inf, sg), axis=0, keepdims=True)
        gs_rows.append(m1 + m2)
    gs = jnp.concatenate(gs_rows, axis=0)
    grow = lax.broadcasted_iota(I32, (N_GROUPS, tm), 0)
    beaten = jnp.zeros((N_GROUPS, tm), F32)
    for g2 in range(N_GROUPS):
        o = gs_rows[g2]
        beats = jnp.where(o > gs, 1.0, jnp.where((o == gs) & (grow > g2), 1.0, 0.0))
        beaten = beaten + beats
    gkeep = jnp.where(beaten < float(TOPK_GROUPS), 1.0, 0.0)
    selm = jnp.concatenate(
        [jnp.where(gkeep[g:g + 1] > 0.5, sel[g * gsz:(g + 1) * gsz], -jnp.inf) for g in range(N_GROUPS)],
        axis=0)
    member = jnp.zeros((n_exp, tm), F32)
    idxs, gates = [], []
    for _ in range(TOP_K):
        m = jnp.max(selm, axis=0, keepdims=True)
        ik = jnp.min(jnp.where(selm == m, eio, float(n_exp)), axis=0, keepdims=True)
        hit = eio == ik
        gates.append(jnp.sum(jnp.where(hit, scores, 0.0), axis=0, keepdims=True))
        idxs.append(ik)
        selm = jnp.where(hit, -jnp.inf, selm)
        member = jnp.where(hit, 1.0, member)
    gsum = gates[0]
    for gk in gates[1:]:
        gsum = gsum + gk
    tr = lax.broadcasted_iota(I32, (tm, tm), 0)
    tc = lax.broadcasted_iota(I32, (tm, tm), 1)
    before = jnp.where(tr < tc, 1.0, 0.0).astype(BF16)
    prefix = _dot(member.astype(BF16), before) + carry[:, :1]
    ranks = [jnp.sum(jnp.where(eio == ik, prefix, 0.0), axis=0, keepdims=True) for ik in idxs]
    carry[...] = carry[...] + jnp.sum(member, axis=1, keepdims=True)
    idx_ref[...] = jnp.concatenate(idxs, axis=0).astype(I32)
    gate_ref[...] = jnp.concatenate([gk / gsum * ROUTED_SCALE for gk in gates], axis=0)
    rank_ref[...] = jnp.concatenate(ranks, axis=0).astype(I32)
    cnt_ref[...] = carry[...].astype(I32)


def _router(h, whi, wlo, eb, cnt0, tm):
    n, d = h.shape
    n_exp = whi.shape[0]
    tok = pl.BlockSpec((TOP_K, tm), lambda i: (0, i))
    full = lambda s: pl.BlockSpec(s, lambda i: (0, 0))
    return pl.pallas_call(
        functools.partial(_router_kernel, n_exp=n_exp, tm=tm),
        grid=(n // tm,),
        in_specs=[pl.BlockSpec((tm, d), lambda i: (i, 0)), full((n_exp, d)), full((n_exp, d)),
                  full((n_exp, 1)), full((n_exp, LANES))],
        out_specs=[tok, tok, tok, full((n_exp, LANES))],
        out_shape=[jax.ShapeDtypeStruct((TOP_K, n), I32), jax.ShapeDtypeStruct((TOP_K, n), F32),
                   jax.ShapeDtypeStruct((TOP_K, n), I32), jax.ShapeDtypeStruct((n_exp, LANES), I32)],
        scratch_shapes=[pltpu.VMEM((n_exp, LANES), F32)],
        compiler_params=_cparams(("arbitrary",)),
    )(h, whi, wlo, eb, cnt0)


def _dest_kernel(idx_ref, rank_ref, ps_ref, o_ref, *, n_exp, tm):
    eio = lax.broadcasted_iota(I32, (n_exp, tm), 0)
    ps = ps_ref[...]
    rows = []
    for k in range(TOP_K):
        hit = eio == idx_ref[k:k + 1, :]
        rows.append(jnp.sum(jnp.where(hit, ps, 0.0), axis=0, keepdims=True))
    o_ref[...] = jnp.concatenate(rows, axis=0).astype(I32) + rank_ref[...]


def _dest(idx, rank, pstart, tm):
    n = idx.shape[1]
    n_exp = pstart.shape[0]
    tok = pl.BlockSpec((TOP_K, tm), lambda i: (0, i))
    return pl.pallas_call(
        functools.partial(_dest_kernel, n_exp=n_exp, tm=tm),
        grid=(n // tm,),
        in_specs=[tok, tok, pl.BlockSpec((n_exp, 1), lambda i: (0, 0))],
        out_specs=tok,
        out_shape=jax.ShapeDtypeStruct((TOP_K, n), I32),
        compiler_params=_cparams(("arbitrary",)),
    )(idx, rank, pstart.astype(F32).reshape(n_exp, 1))


def _zero_pads_kernel(last_ref, has_ref, xs_ref, zbuf, sem):
    e = pl.program_id(0)
    n_exp = pl.num_programs(0)

    def zero_copy(start):
        return pltpu.make_async_copy(zbuf, xs_ref.at[pl.ds(pl.multiple_of(start, EXPERT_ROWS), EXPERT_ROWS)], sem)

    @pl.when(e == 0)
    def _():
        zbuf[...] = jnp.zeros_like(zbuf)

    @pl.when(has_ref[e] > 0)
    def _():
        zero_copy(last_ref[e]).start()

    @pl.when(e == n_exp - 1)
    def _():
        def drain(j, carry):
            @pl.when(has_ref[j] > 0)
            def _():
                zero_copy(0).wait()
            return carry
        lax.fori_loop(0, n_exp, drain, 0)


def _zero_pads(last_start, has_rows, n_rows, half):
    n_exp = last_start.shape[0]
    s = half // LANES
    grid_spec = pltpu.PrefetchScalarGridSpec(
        num_scalar_prefetch=2,
        grid=(n_exp,),
        in_specs=[],
        out_specs=pl.BlockSpec(memory_space=pl.ANY),
        scratch_shapes=[pltpu.VMEM((EXPERT_ROWS, s, LANES), U32), pltpu.SemaphoreType.DMA(())],
    )
    return pl.pallas_call(
        _zero_pads_kernel,
        grid_spec=grid_spec,
        out_shape=jax.ShapeDtypeStruct((n_rows, s, LANES), U32),
        compiler_params=_cparams(("arbitrary",)),
    )(last_start, has_rows)


def _dispatch_kernel(dest_ref, hp_ref, xs_in_ref, xs_ref, dsm, sem_d, sem, *, tm):
    del xs_in_ref
    i = pl.program_id(0)
    cp = pltpu.make_async_copy(dest_ref.at[i], dsm, sem_d)
    cp.start()
    cp.wait()

    def row_copy(t, d):
        return pltpu.make_async_copy(hp_ref.at[t], xs_ref.at[d], sem)

    def issue(t, carry):
        for k in range(TOP_K):
            row_copy(t, dsm[k, t]).start()
        return carry

    lax.fori_loop(0, tm, issue, 0)

    def drain(t, carry):
        for k in range(TOP_K):
            row_copy(0, 0).wait()
        return carry

    lax.fori_loop(0, tm, drain, 0)


def _dispatch(dest_tiles, hp, xs, tm):
    n, s, _ = hp.shape
    return pl.pallas_call(
        functools.partial(_dispatch_kernel, tm=tm),
        grid=(n // tm,),
        in_specs=[pl.BlockSpec(memory_space=pl.ANY),
                  pl.BlockSpec((tm, s, LANES), lambda i: (i, 0, 0)),
                  pl.BlockSpec(memory_space=pl.ANY)],
        out_specs=pl.BlockSpec(memory_space=pl.ANY),
        out_shape=jax.ShapeDtypeStruct(xs.shape, U32),
        scratch_shapes=[pltpu.SMEM((TOP_K, tm), I32), pltpu.SemaphoreType.DMA(()), pltpu.SemaphoreType.DMA(())],
        input_output_aliases={2: 0},
        compiler_params=_cparams(("arbitrary",)),
    )(dest_tiles, hp, xs)


def _experts_kernel(be_ref, nu_ref, nxt_ref, x_ref, wg_hbm, wu_hbm, wd_hbm, y_ref,
                    wg_f, wu_f, wd_f, wg_s, wu_s, wd_s, slot_ref, sem, *, half, n_exp):
    b = pl.program_id(0)
    active = b < nu_ref[0]
    e = be_ref[b]
    new_expert = (b == 0) | (e != be_ref[jnp.maximum(b - 1, 0)])

    def weight_copies(expert, s):
        return (pltpu.make_async_copy(wg_hbm.at[expert], wg_f.at[s], sem.at[s]),
                pltpu.make_async_copy(wu_hbm.at[expert], wu_f.at[s], sem.at[s]),
                pltpu.make_async_copy(wd_hbm.at[expert], wd_f.at[s], sem.at[s]))

    @pl.when(active & (b == 0))
    def _():
        slot_ref[0] = 0
        for cp in weight_copies(e, 0):
            cp.start()

    @pl.when(active & new_expert)
    def _():
        s = slot_ref[0]
        for cp in weight_copies(e, s):
            cp.wait()
        nxt = nxt_ref[e]

        @pl.when(nxt < n_exp)
        def _():
            for cp in weight_copies(nxt, 1 - s):
                cp.start()

        wg_s[...] = wg_f[s].astype(BF16)
        wu_s[...] = wu_f[s].astype(BF16)
        wd_s[...] = wd_f[s].astype(BF16)
        slot_ref[0] = 1 - s

    @pl.when(active)
    def _():
        lo, hi = _unpack_pairs(_load_row_tiled(x_ref, 0, EXPERT_ROWS, half // LANES))
        lo = lo.astype(BF16)
        hi = hi.astype(BF16)
        g = _dot(lo, wg_s[:half, :]) + _dot(hi, wg_s[half:, :])
        u = _dot(lo, wu_s[:half, :]) + _dot(hi, wu_s[half:, :])
        hm = (_silu(g) * u).astype(BF16)
        _store_row_tiled(y_ref, _pack_pairs(_dot(hm, wd_s[...])))


def _experts(block_e, n_used, next_expert, xs, wg, wu, wd):
    n_exp, d, de = wg.shape
    half = d // 2
    s = half // LANES
    n_rows = xs.shape[0] // s
    nb = n_rows // EXPERT_ROWS
    blk = lambda b, be, nu, nx: (jnp.minimum(b, nu[0] - 1), 0)
    hbm = pl.BlockSpec(memory_space=pl.ANY)
    grid_spec = pltpu.PrefetchScalarGridSpec(
        num_scalar_prefetch=3,
        grid=(nb,),
        in_specs=[pl.BlockSpec((EXPERT_ROWS * s, LANES), blk), hbm, hbm, hbm],
        out_specs=pl.BlockSpec((EXPERT_ROWS * s, LANES), blk),
        scratch_shapes=[pltpu.VMEM((2, d, de), F32), pltpu.VMEM((2, d, de), F32), pltpu.VMEM((2, de, d), F32),
                        pltpu.VMEM((d, de), BF16), pltpu.VMEM((d, de), BF16), pltpu.VMEM((de, d), BF16),
                        pltpu.SMEM((1,), I32), pltpu.SemaphoreType.DMA((2,))],
    )
    return pl.pallas_call(
        functools.partial(_experts_kernel, half=half, n_exp=n_exp),
        grid_spec=grid_spec,
        out_shape=jax.ShapeDtypeStruct((n_rows * s, LANES), U32),
        compiler_params=_cparams(("arbitrary",)),
    )(block_e, n_used, next_expert, xs, wg, wu, wd)


def _combine_kernel(dest_ref, h_ref, gate_ref, ys_ref, wsg_ref, wsu_ref, wsd_ref, g_ref, b_ref, y_ref,
                    buf, dsm, sem_d, sem, *, tm, alpha):
    i = pl.program_id(0)
    last = pl.num_programs(0) - 1
    slot = i % 2
    nslot = 1 - slot

    def table_copy(tile, s):
        return pltpu.make_async_copy(dest_ref.at[jnp.minimum(tile, last)], dsm.at[s], sem_d.at[s])

    rs = ys_ref.shape[1]

    def row_copy(s, k, t, d):
        start = (k * tm + t) * rs
        if not isinstance(start, int):
            start = pl.multiple_of(start, rs)
        return pltpu.make_async_copy(ys_ref.at[d], buf.at[s, pl.ds(start, rs)], sem.at[s])

    def wait_rows(s):
        def drain(t, carry):
            for k in range(TOP_K):
                row_copy(s, 0, 0, 0).wait()
            return carry
        lax.fori_loop(0, tm, drain, 0)

    @pl.when(i == 0)
    def _():
        table_copy(0, 0).start()
        table_copy(0, 0).wait()

        def issue(t, carry):
            for k in range(TOP_K):
                row_copy(0, k, t, dsm[0, k, t]).start()
            return carry
        lax.fori_loop(0, tm, issue, 0)
        table_copy(1, 1).start()

    table_copy(i + 1, nslot).wait()
    wait_rows(slot)
    table_copy(i + 2, slot).start()

    for t in range(tm):
        for k in range(TOP_K):
            row_copy(nslot, k, t, dsm[nslot, k, t]).start()

    h = h_ref[...]
    hb = h.astype(BF16)
    sh = _dot((_silu(_dot(hb, wsg_ref[...])) * _dot(hb, wsu_ref[...])).astype(BF16), wsd_ref[...])

    gate = gate_ref[...]
    rows = buf.at[slot]
    acc_lo = jnp.zeros((tm, rs * LANES), F32)
    acc_hi = jnp.zeros((tm, rs * LANES), F32)
    for k in range(TOP_K):
        lo, hi = _unpack_pairs(_load_row_tiled(rows, k * tm * rs, tm, rs))
        gk = gate[:, k:k + 1]
        acc_lo = acc_lo + gk * lo
        acc_hi = acc_hi + gk * hi
    routed = jnp.concatenate([acc_lo, acc_hi], axis=1)
    y_ref[...] = _layer_norm(alpha * h + (routed + sh), g_ref[...], b_ref[...])

    @pl.when(i == last)
    def _():
        wait_rows(nslot)
        table_copy(i + 2, slot).wait()


def _combine(dest_tiles, h, gate_t, ys, wsg, wsu, wsd, g, b, alpha, tm):
    n, d = h.shape
    rs = ys.shape[1]
    full = lambda a: pl.BlockSpec(a.shape, lambda i: (0, 0))
    return pl.pallas_call(
        functools.partial(_combine_kernel, tm=tm, alpha=alpha),
        grid=(n // tm,),
        in_specs=[pl.BlockSpec(memory_space=pl.ANY),
                  pl.BlockSpec((tm, d), lambda i: (i, 0)),
                  pl.BlockSpec((tm, TOP_K), lambda i: (i, 0)),
                  pl.BlockSpec(memory_space=pl.ANY),
                  full(wsg), full(wsu), full(wsd), full(g), full(b)],
        out_specs=pl.BlockSpec((tm, d), lambda i: (i, 0)),
        out_shape=jax.ShapeDtypeStruct((n, d), F32),
        scratch_shapes=[pltpu.VMEM((2, TOP_K * tm * rs, LANES), U32), pltpu.SMEM((2, TOP_K, tm), I32),
                        pltpu.SemaphoreType.DMA((2,)), pltpu.SemaphoreType.DMA((2,))],
        compiler_params=_cparams(("arbitrary",)),
    )(dest_tiles, h, gate_t, ys, wsg, wsu, wsd, g, b)


def _rope_tables(pos):
    half = HEAD_DIM // 2
    inv_freq = ROPE_BASE ** (-jnp.arange(half, dtype=F32) / half)
    ang = pos.astype(F32)[:, None] * inv_freq[None, :]
    cos, sin = jnp.cos(ang), jnp.sin(ang)
    return jnp.concatenate([cos, cos], axis=-1), jnp.concatenate([-sin, sin], axis=-1)


def _dest_tiles(dest, tm):
    k, n = dest.shape
    return dest.reshape(k, n // tm, tm).transpose(1, 0, 2)


def _layer(xp, xs, state_ret, cache_k, cache_v, cache_logf, w_in, b_fgate, ret_gn_g, w_out,
           ln1_g, ln1_b, w_router, e_bias, w_e_gate, w_e_up, w_e_down, w_s_gate, w_s_up, w_s_down,
           ln2_g, ln2_b, alpha):
    bp, tp, d = xp.shape
    bs, ts, _ = xs.shape
    past = cache_k.shape[1]
    width = w_out.shape[0] // 2
    n_heads = width // HEAD_DIM
    n_exp = w_router.shape[1]

    w_ret = w_in[:, :4 * width].astype(BF16)
    w_fox = w_in[:, 4 * width:7 * width].astype(BF16)
    n_f = w_in.shape[1] - 7 * width
    w_f = jnp.pad(w_in[:, 7 * width:], ((0, 0), (0, LANES - n_f))).astype(BF16)
    b_f = jnp.pad(b_fgate, (0, LANES - n_f)).reshape(1, LANES)
    gn = ret_gn_g.reshape(1, width)
    w_o = w_out.astype(BF16)
    l1g, l1b = ln1_g.reshape(1, d), ln1_b.reshape(1, d)
    l2g, l2b = ln2_g.reshape(1, d), ln2_b.reshape(1, d)
    wr_t = w_router.T
    wr_top = _truncate_to_bf16(wr_t)
    wr_hi = wr_top.astype(BF16)
    wr_lo = (wr_t - wr_top).astype(BF16)
    eb = e_bias.reshape(n_exp, 1)
    wsg, wsu, wsd = w_s_gate.astype(BF16), w_s_up.astype(BF16), w_s_down.astype(BF16)

    xp2 = xp.reshape(bp * tp, d)
    tm_p = _tile(tp, 512)
    cs_p, sn_p = _rope_tables(jnp.arange(tp))
    pr = _proj_ret(xp2, w_ret, cs_p, sn_p, tp, tm_p).reshape(bp, tp, 4 * width)
    fk, fv, lf, c_p, qa_p, ka_p, vt_p = _proj_fox_heads(xp2, w_fox, w_f, b_f, bp, _tile(tp, 256))
    s0 = jnp.zeros((bp, n_heads, HEAD_DIM, HEAD_DIM), F32)
    ry_p, sfin_p = _retention(pr, s0, gn, _tile(tp, 256))
    lf_p = lf.reshape(bp, tp, LANES)
    tq_p = _tile(tp, 2048)
    skip_p = _fox_skip_table(qa_p, ka_p, c_p.reshape(bp, tp, LANES), tq_p, tq_p // 2)
    fo_p = _fox(qa_p, ka_p, vt_p, 0, tq_p, tq_p // 2, n_chains=2, skip=skip_p)
    h_p, hp_p = _finish(xp2, ry_p.reshape(bp * tp, width), fo_p.reshape(bp * tp, width), w_o, l1g, l1b,
                        alpha, _tile(bp * tp, 256))

    xs2 = xs.reshape(bs * ts, d)
    tm_s = _tile(ts, 512)
    cs_s, sn_s = _rope_tables(past + jnp.arange(ts))
    prs = _proj_ret(xs2, w_ret, cs_s, sn_s, ts, tm_s).reshape(bs, ts, 4 * width)
    fq_s, fk_s, fv_s, fkb_s, fvb_s, lf_sn = _proj_fox(xs2, w_fox, w_f, b_f, tm_s)
    ry_s, sfin_s = _retention(prs, state_ret.astype(F32), gn, ts)
    tk_s = LANES
    tq_s = -(-ts // LANES) * LANES
    t_all = -(-(past + tq_s) // tk_s) * tk_s
    pad_t = t_all - past - ts
    lf_s3 = lf_sn.reshape(bs, ts, LANES)
    lf_all = jnp.concatenate([
        jnp.pad(cache_logf.astype(F32), ((0, 0), (0, 0), (0, LANES - n_heads))),
        lf_s3, jnp.zeros((bs, pad_t, LANES), F32)], axis=1)
    c_s = _cumsum(lf_all, tk_s)
    zpad = jnp.zeros((bs, pad_t, width), BF16)
    k_all = jnp.concatenate([cache_k.reshape(bs, past, width).astype(BF16),
                             fkb_s.reshape(bs, ts, width), zpad], axis=1)
    v_all = jnp.concatenate([cache_v.reshape(bs, past, width).astype(BF16),
                             fvb_s.reshape(bs, ts, width), zpad], axis=1)
    qa_s, ka_s, vt_s = _fox_operands(fq_s.reshape(bs, ts, width), k_all, v_all, c_s, tq_s, past)
    fo_s = _fox(qa_s, ka_s, vt_s, past, tq_s, t_all)[:, :ts]
    h_s, hp_s = _finish(xs2, ry_s.reshape(bs * ts, width), fo_s.reshape(bs * ts, width), w_o, l1g, l1b,
                        alpha, _tile(bs * ts, 256))

    n_p, n_s = bp * tp, bs * ts
    tm_r = 256
    cnt0 = jnp.zeros((n_exp, LANES), I32)
    idx_p, gate_p, rank_p, cnt1 = _router(h_p, wr_hi, wr_lo, eb, cnt0, _tile(n_p, tm_r))
    idx_s, gate_s, rank_s, cnt2 = _router(h_s, wr_hi, wr_lo, eb, cnt1, _tile(n_s, tm_r))
    counts = cnt2[:, 0]
    padded = (counts + EXPERT_ROWS - 1) // EXPERT_ROWS * EXPERT_ROWS
    pend = jnp.cumsum(padded)
    pstart = pend - padded
    dest_p = _dest(idx_p, rank_p, pstart, _tile(n_p, 512))
    dest_s = _dest(idx_s, rank_s, pstart, _tile(n_s, 512))
    n_blocks = -(-((n_p + n_s) * TOP_K) // EXPERT_ROWS) + n_exp
    n_used = (pend[-1] // EXPERT_ROWS).astype(I32).reshape(1)
    block_start = jnp.arange(n_blocks, dtype=I32) * EXPERT_ROWS
    block_e = jnp.minimum(jnp.sum((pend[None, :] <= block_start[:, None]).astype(I32), axis=1),
                          n_exp - 1).astype(I32)
    tm_d = 256
    dt_p = _dest_tiles(dest_p, _tile(n_p, tm_d))
    dt_s = _dest_tiles(dest_s, _tile(n_s, tm_d))
    rs = d // 2 // LANES
    n_rows = n_blocks * EXPERT_ROWS
    xs_rows = _zero_pads((pend - EXPERT_ROWS).astype(I32), padded.astype(I32), n_rows, d // 2)
    xs_rows = _dispatch(dt_p, hp_p.reshape(n_p, rs, LANES), xs_rows, _tile(n_p, tm_d))
    xs_rows = _dispatch(dt_s, hp_s.reshape(n_s, rs, LANES), xs_rows, _tile(n_s, tm_d))
    xs_rows = xs_rows.reshape(n_rows * rs, LANES)
    owners = jnp.where(counts > 0, jnp.arange(n_exp, dtype=I32), n_exp)
    later = lax.cummin(owners[::-1])[::-1]
    next_expert = jnp.concatenate([later[1:], jnp.full((1,), n_exp, I32)]).astype(I32)
    ys_rows = _experts(block_e, n_used, next_expert, xs_rows, w_e_gate, w_e_up, w_e_down)
    ys_rows = ys_rows.reshape(n_rows, rs, LANES)
    tm_c = 128
    y_p = _combine(_dest_tiles(dest_p, _tile(n_p, tm_c)), h_p, gate_p.T, ys_rows, wsg, wsu, wsd, l2g, l2b,
                   alpha, _tile(n_p, tm_c))
    y_s = _combine(_dest_tiles(dest_s, _tile(n_s, tm_c)), h_s, gate_s.T, ys_rows, wsg, wsu, wsd, l2g, l2b,
                   alpha, _tile(n_s, tm_c))

    outs_p = (sfin_p, fk.reshape(bp, tp, n_heads, HEAD_DIM), fv.reshape(bp, tp, n_heads, HEAD_DIM),
              lf_p[:, :, :n_heads])
    outs_s = (sfin_s, fk_s.reshape(bs, ts, n_heads, HEAD_DIM), fv_s.reshape(bs, ts, n_heads, HEAD_DIM),
              lf_s3[:, :, :n_heads])
    return y_p.reshape(bp, tp, d), y_s.reshape(bs, ts, d), outs_p, outs_s


def kernel(x_prompt, x_sample, state_ret, cache_fox_k, cache_fox_v, cache_fox_logf, w_in, b_fgate, ret_gn_g, w_out, ln1_g, ln1_b, w_router, e_bias, w_e_gate, w_e_up, w_e_down, w_s_gate, w_s_up, w_s_down, ln2_g, ln2_b):
    depth = w_in.shape[0]
    alpha = (2.0 * depth) ** 0.25
    xp, xs = x_prompt, x_sample
    per_p, per_s = [], []
    for l in range(depth):
        xp, xs, op, os_ = _layer(
            xp, xs, state_ret[l], cache_fox_k[l], cache_fox_v[l], cache_fox_logf[l], w_in[l], b_fgate[l],
            ret_gn_g[l], w_out[l], ln1_g[l], ln1_b[l], w_router[l], e_bias[l], w_e_gate[l], w_e_up[l],
            w_e_down[l], w_s_gate[l], w_s_up[l], w_s_down[l], ln2_g[l], ln2_b[l], alpha)
        per_p.append(op)
        per_s.append(os_)
    stack = lambda items, j: jnp.stack([it[j] for it in items])
    return (xp, xs,
            stack(per_p, 0), stack(per_p, 1), stack(per_p, 2), stack(per_p, 3),
            stack(per_s, 0).astype(state_ret.dtype), stack(per_s, 1), stack(per_s, 2),
            stack(per_s, 3).astype(cache_fox_logf.dtype))
```

```python
import functools
import math

import jax
import jax.numpy as jnp
from jax import lax
from jax.experimental import pallas as pl
from jax.experimental.pallas import tpu as pltpu

HEAD_DIM = 128
ROPE_BASE = 10000.0
N_GROUPS = 8
TOPK_GROUPS = 4
TOP_K = 8
ROUTED_SCALE = 2.5
LN_EPS = 1e-5
LANES = 128
EXPERT_ROWS = 256
LOG2_E = math.log2(math.e)
VMEM_LIMIT_BYTES = 56 * 1024 * 1024

F32 = jnp.float32
BF16 = jnp.bfloat16
U32 = jnp.uint32
I32 = jnp.int32


def _cparams(sem):
    return pltpu.CompilerParams(dimension_semantics=sem, vmem_limit_bytes=VMEM_LIMIT_BYTES)


def _tile(n, pref):
    t = min(n, pref)
    while n % t:
        t -= 1
    return t


def _dot(a, b):
    return jnp.dot(a, b, preferred_element_type=F32)


def _dot_nt(a, b):
    return lax.dot_general(a, b, (((1,), (1,)), ((), ())), preferred_element_type=F32)


def _dot_tn(a, b):
    return lax.dot_general(a, b, (((0,), (0,)), ((), ())), preferred_element_type=F32)


def _pack_pairs(a):
    n = a.shape[1] // 2
    lo = lax.bitcast_convert_type(a[:, :n].astype(BF16).astype(F32), U32)
    hi = lax.bitcast_convert_type(a[:, n:].astype(BF16).astype(F32), U32)
    return (lo >> 16) | (hi & jnp.uint32(0xFFFF0000))


def _unpack_pairs(u):
    lo = lax.bitcast_convert_type(u << 16, F32)
    hi = lax.bitcast_convert_type(u & jnp.uint32(0xFFFF0000), F32)
    return lo, hi


def _layer_norm(z, g, b):
    mu = jnp.mean(z, axis=-1, keepdims=True)
    zc = z - mu
    var = jnp.mean(zc * zc, axis=-1, keepdims=True)
    return zc * lax.rsqrt(var + LN_EPS) * g + b


def _silu(g):
    return g * jax.nn.sigmoid(g)


def _proj_ret_kernel(x_ref, w_ref, cs_ref, sn_ref, o_ref, *, width, scale):
    xb = x_ref[...].astype(BF16)
    cs = cs_ref[...]
    sn = sn_ref[...]
    for sec in range(4):
        p = _dot(xb, w_ref[:, sec * width:(sec + 1) * width])
        if sec < 2:
            for h in range(width // HEAD_DIM):
                ph = p[:, h * HEAD_DIM:(h + 1) * HEAD_DIM]
                r = ph * cs + pltpu.roll(ph, HEAD_DIM // 2, 1) * sn
                if sec == 1:
                    r = r * scale
                o_ref[:, sec * width + h * HEAD_DIM:sec * width + (h + 1) * HEAD_DIM] = r.astype(BF16)
        else:
            o_ref[:, sec * width:(sec + 1) * width] = p.astype(BF16)


def _proj_ret(x, w, cs, sn, seq, tm):
    n, d = x.shape
    width = w.shape[1] // 4
    nt_seq = seq // tm
    return pl.pallas_call(
        functools.partial(_proj_ret_kernel, width=width, scale=HEAD_DIM ** -0.5),
        grid=(n // tm,),
        in_specs=[
            pl.BlockSpec((tm, d), lambda i: (i, 0)),
            pl.BlockSpec((d, 4 * width), lambda i: (0, 0)),
            pl.BlockSpec((tm, HEAD_DIM), lambda i: (i % nt_seq, 0)),
            pl.BlockSpec((tm, HEAD_DIM), lambda i: (i % nt_seq, 0)),
        ],
        out_specs=pl.BlockSpec((tm, 4 * width), lambda i: (i, 0)),
        out_shape=jax.ShapeDtypeStruct((n, 4 * width), BF16),
        compiler_params=_cparams(("arbitrary",)),
    )(x, w, cs, sn)


def _proj_fox_kernel(x_ref, w_ref, wf_ref, bf_ref, q_ref, k_ref, v_ref, kb_ref, vb_ref, lf_ref,
                     *, width, scale):
    xb = x_ref[...].astype(BF16)
    q = _dot(xb, w_ref[:, :width])
    q_ref[...] = (q * scale).astype(BF16)
    k = _dot(xb, w_ref[:, width:2 * width])
    k_ref[...] = k
    kb_ref[...] = k.astype(BF16)
    v = _dot(xb, w_ref[:, 2 * width:])
    v_ref[...] = v
    vb_ref[...] = v.astype(BF16)
    z = _dot(xb, wf_ref[...]) + bf_ref[...]
    lf_ref[...] = jnp.minimum(z, 0.0) - jnp.log1p(jnp.exp(-jnp.abs(z)))


def _log_sigmoid(z):
    return jnp.minimum(z, 0.0) - jnp.log1p(jnp.exp(-jnp.abs(z)))


def _proj_fox_heads_kernel(x_ref, w_ref, wf_ref, bf_ref, k_ref, v_ref, lf_ref, c_ref, qa_ref, ka_ref, vt_ref,
                           nrm_ref, carry, *, width, scale, n_heads):
    @pl.when(pl.program_id(1) == 0)
    def _():
        carry[...] = jnp.zeros_like(carry)

    tm = x_ref.shape[0]
    xb = x_ref[...].astype(BF16)
    q = _dot(xb, w_ref[:, :width]) * scale
    k = _dot(xb, w_ref[:, width:2 * width])
    v = _dot(xb, w_ref[:, 2 * width:])
    k_ref[...] = k
    v_ref[...] = v
    logf = _log_sigmoid(_dot(xb, wf_ref[...]) + bf_ref[...])
    lf_ref[...] = logf
    c = _tile_cumsum(logf, carry[...])
    c_ref[...] = c
    carry[...] = c[tm - 1:tm, :]
    c2 = c * LOG2_E
    lane = lax.broadcasted_iota(I32, (tm, HEAD_DIM), 1)
    lane_row = lax.broadcasted_iota(I32, (1, LANES), 1)

    def max_sq_norm(a):
        af = a.astype(F32)
        return jnp.max(jnp.sum(af * af, axis=-1, keepdims=True), axis=0, keepdims=True)

    qn = jnp.zeros((1, LANES), F32)
    kn = jnp.zeros((1, LANES), F32)
    for h in range(n_heads):
        sl = slice(h * HEAD_DIM, (h + 1) * HEAD_DIM)
        hi, mid, lo = _split3(c2[:, h:h + 1])
        terms = jnp.where(lane == 0, hi.astype(F32), jnp.where(lane == 1, mid.astype(F32), lo.astype(F32)))
        q_tail = jnp.where(lane < 3, terms, jnp.where(lane < 6, 1.0, 0.0))
        k_tail = jnp.where(lane < 3, 1.0, jnp.where(lane < 6, -pltpu.roll(terms, 3, 1), 0.0))
        qh = q[:, sl].astype(BF16)
        kh = k[:, sl].astype(BF16)
        qa_ref[0, h, :, :HEAD_DIM] = qh
        qa_ref[0, h, :, HEAD_DIM:] = q_tail.astype(BF16)
        ka_ref[0, h, :, :HEAD_DIM] = kh
        ka_ref[0, h, :, HEAD_DIM:] = k_tail.astype(BF16)
        vt_ref[0, h] = v[:, sl].T.astype(BF16)
        qn = jnp.where(lane_row == h, max_sq_norm(qh), qn)
        kn = jnp.where(lane_row == h, max_sq_norm(kh), kn)

    @pl.when(pl.program_id(1) == 0)
    def _():
        nrm_ref[...] = jnp.zeros_like(nrm_ref)

    nrm_ref[0, 0:1, :] = jnp.maximum(nrm_ref[0, 0:1, :], qn)
    nrm_ref[0, 1:2, :] = jnp.maximum(nrm_ref[0, 1:2, :], kn)


def _proj_fox_heads(x, w, wf, bfg, batch, tm):
    n, d = x.shape
    width = w.shape[1] // 3
    n_heads = width // HEAD_DIM
    seq = n // batch
    nt = seq // tm
    row = lambda c: pl.BlockSpec((tm, c), lambda b, i: (b * nt + i, 0))
    const = lambda s: pl.BlockSpec(s, lambda b, i: (0, 0))
    aug = pl.BlockSpec((1, n_heads, tm, 2 * HEAD_DIM), lambda b, i: (b, 0, i, 0))
    return pl.pallas_call(
        functools.partial(_proj_fox_heads_kernel, width=width, scale=HEAD_DIM ** -0.5 * LOG2_E,
                          n_heads=n_heads),
        grid=(batch, nt),
        in_specs=[row(d), const((d, 3 * width)), const((d, LANES)), const((1, LANES))],
        out_specs=[row(width), row(width), row(LANES), row(LANES), aug, aug,
                   pl.BlockSpec((1, n_heads, HEAD_DIM, tm), lambda b, i: (b, 0, 0, i)),
                   pl.BlockSpec((1, 8, LANES), lambda b, i: (b, 0, 0))],
        out_shape=[
            jax.ShapeDtypeStruct((n, width), F32),
            jax.ShapeDtypeStruct((n, width), F32),
            jax.ShapeDtypeStruct((n, LANES), F32),
            jax.ShapeDtypeStruct((n, LANES), F32),
            jax.ShapeDtypeStruct((batch, n_heads, seq, 2 * HEAD_DIM), BF16),
            jax.ShapeDtypeStruct((batch, n_heads, seq, 2 * HEAD_DIM), BF16),
            jax.ShapeDtypeStruct((batch, n_heads, HEAD_DIM, seq), BF16),
            jax.ShapeDtypeStruct((batch, 8, LANES), F32),
        ],
        scratch_shapes=[pltpu.VMEM((1, LANES), F32)],
        compiler_params=_cparams(("arbitrary", "arbitrary")),
    )(x, w, wf, bfg)


def _proj_fox(x, w, wf, bfg, tm):
    n, d = x.shape
    width = w.shape[1] // 3
    row = lambda c: pl.BlockSpec((tm, c), lambda i: (i, 0))
    return pl.pallas_call(
        functools.partial(_proj_fox_kernel, width=width, scale=HEAD_DIM ** -0.5 * LOG2_E),
        grid=(n // tm,),
        in_specs=[
            row(d),
            pl.BlockSpec((d, 3 * width), lambda i: (0, 0)),
            pl.BlockSpec((d, LANES), lambda i: (0, 0)),
            pl.BlockSpec((1, LANES), lambda i: (0, 0)),
        ],
        out_specs=[row(width), row(width), row(width), row(width), row(width), row(LANES)],
        out_shape=[
            jax.ShapeDtypeStruct((n, width), BF16),
            jax.ShapeDtypeStruct((n, width), F32),
            jax.ShapeDtypeStruct((n, width), F32),
            jax.ShapeDtypeStruct((n, width), BF16),
            jax.ShapeDtypeStruct((n, width), BF16),
            jax.ShapeDtypeStruct((n, LANES), F32),
        ],
        compiler_params=_cparams(("arbitrary",)),
    )(x, w, wf, bfg)


def _split3(x):
    hi = x.astype(BF16)
    r1 = x - hi.astype(F32)
    mid = r1.astype(BF16)
    lo = (r1 - mid.astype(F32)).astype(BF16)
    return hi, mid, lo


def _tile_cumsum(x, carry_row):
    tm = x.shape[0]
    r = lax.broadcasted_iota(I32, (tm, tm), 0)
    c = lax.broadcasted_iota(I32, (tm, tm), 1)
    tri = jnp.where(c <= r, 1.0, 0.0).astype(BF16)
    hi, mid, lo = _split3(x)
    return _dot(tri, hi) + _dot(tri, mid) + _dot(tri, lo) + carry_row


def _cumsum_kernel(x_ref, o_ref, carry, *, tm):
    @pl.when(pl.program_id(1) == 0)
    def _():
        carry[...] = jnp.zeros_like(carry)

    out = _tile_cumsum(x_ref[0], carry[...])
    o_ref[0] = out
    carry[...] = out[tm - 1:tm, :]


def _cumsum(x, tm):
    b, t, _ = x.shape
    return pl.pallas_call(
        functools.partial(_cumsum_kernel, tm=tm),
        grid=(b, t // tm),
        in_specs=[pl.BlockSpec((1, tm, LANES), lambda i, j: (i, j, 0))],
        out_specs=pl.BlockSpec((1, tm, LANES), lambda i, j: (i, j, 0)),
        out_shape=jax.ShapeDtypeStruct(x.shape, F32),
        scratch_shapes=[pltpu.VMEM((1, LANES), F32)],
        compiler_params=_cparams(("arbitrary", "arbitrary")),
    )(x)


def _retention_kernel(q_ref, k_ref, v_ref, g_ref, s0_ref, gn_ref, y_ref, sout_ref, s_scr,
                      *, n_heads, chunk):
    c = pl.program_id(1)

    @pl.when(c == 0)
    def _():
        s_scr[...] = s0_ref[0]

    row = lax.broadcasted_iota(I32, (chunk, chunk), 0)
    col = lax.broadcasted_iota(I32, (chunk, chunk), 1)
    rel = (row - col).astype(F32)
    ri = lax.broadcasted_iota(I32, (chunk, HEAD_DIM), 0).astype(F32)
    for h in range(n_heads):
        sl = slice(h * HEAD_DIM, (h + 1) * HEAD_DIM)
        lg = math.log1p(-(2.0 ** (-5 - h)))
        decay = jnp.where(rel >= 0, jnp.exp(lg * jnp.maximum(rel, 0.0)), 0.0)
        q = q_ref[0, :, sl]
        k = k_ref[0, :, sl]
        v = v_ref[0, :, sl]
        state = s_scr[h]
        scores = _dot_nt(q, k) * decay
        o = _dot(scores.astype(BF16), v)
        o = o + jnp.exp(lg * (ri + 1.0)) * _dot(q, state.astype(BF16))
        kd = (k.astype(F32) * jnp.exp(lg * (chunk - 1.0 - ri))).astype(BF16)
        s_scr[h] = math.exp(lg * chunk) * state + _dot_tn(kd, v)
        mu = jnp.mean(o, axis=-1, keepdims=True)
        oc = o - mu
        var = jnp.mean(oc * oc, axis=-1, keepdims=True)
        yn = oc * lax.rsqrt(var + LN_EPS) * gn_ref[:, sl]
        y_ref[0, :, sl] = (_silu(g_ref[0, :, sl].astype(F32)) * yn).astype(BF16)

    @pl.when(c == pl.num_programs(1) - 1)
    def _():
        sout_ref[0] = s_scr[...]


def _retention(p, s0, gn, chunk):
    b, t, w4 = p.shape
    width = w4 // 4
    n_heads = width // HEAD_DIM
    sec = lambda s: pl.BlockSpec((1, chunk, width), lambda i, j: (i, j, s))
    st = pl.BlockSpec((1, n_heads, HEAD_DIM, HEAD_DIM), lambda i, j: (i, 0, 0, 0))
    return pl.pallas_call(
        functools.partial(_retention_kernel, n_heads=n_heads, chunk=chunk),
        grid=(b, t // chunk),
        in_specs=[sec(0), sec(1), sec(2), sec(3), st, pl.BlockSpec((1, width), lambda i, j: (0, 0))],
        out_specs=[pl.BlockSpec((1, chunk, width), lambda i, j: (i, j, 0)), st],
        out_shape=[jax.ShapeDtypeStruct((b, t, width), BF16),
                   jax.ShapeDtypeStruct(s0.shape, F32)],
        scratch_shapes=[pltpu.VMEM((n_heads, HEAD_DIM, HEAD_DIM), F32)],
        compiler_params=_cparams(("arbitrary", "arbitrary")),
    )(p, p, p, p, s0, gn)


def _fox_kernel(skip_ref, q_ref, k_ref, vt_ref, o_ref, *, tq, tk, q_off, n_kblocks, n_chains):
    qi = pl.program_id(2)
    tqc = tq // n_chains

    def absorb(s, q_first, vt, k0, carry, masked):
        m, l, acc = carry
        if masked:
            kpos = k0 + lax.broadcasted_iota(I32, (tk, tqc), 0)
            qpos = q_first + lax.broadcasted_iota(I32, (tk, tqc), 1)
            s = jnp.where(kpos <= qpos, s, -jnp.inf)
        m_new = jnp.maximum(m, jnp.max(s, axis=0, keepdims=True))
        alpha = jnp.exp2(m - m_new)
        p = jnp.exp2(s - m_new)
        l = alpha * l + jnp.sum(p, axis=0, keepdims=True)
        acc = alpha * acc + _dot(vt, p.astype(BF16))
        return m_new, l, acc

    def kv_block(j):
        k0 = pl.multiple_of(j * tk, tk)
        return k_ref[0, 0, pl.ds(k0, tk), :], vt_ref[0, 0, :, pl.ds(k0, tk)], k0

    init =(jnp.full((1, tqc), -jnp.inf, F32), jnp.zeros((1, tqc), F32), jnp.zeros((HEAD_DIM, tqc), F32))

    if n_chains == 1:
        q = q_ref[0, 0]
        q_first = q_off + qi * tq
        n_full = jnp.minimum((q_first + 1) // tk, n_kblocks)
        n_tot = jnp.minimum((q_first + tq + tk - 1) // tk, n_kblocks)

        def step(j, carry, masked):
            k, vt, k0 = kv_block(j)
            return absorb(_dot_nt(k, q), q_first, vt, k0, carry, masked)

        carry = lax.fori_loop(0, n_full, functools.partial(step, masked=False), init)
        _, l, acc = lax.fori_loop(n_full, n_tot, functools.partial(step, masked=True), carry)
        o_ref[0] = (acc / l).T.astype(BF16)
    else:
        qs = [q_ref[0, 0, c * tqc:(c + 1) * tqc, :] for c in range(n_chains)]
        firsts = [qi * tq + c * tqc for c in range(n_chains)]

        def step(j, carries):
            k, vt, k0 = kv_block(j)
            scores = [_dot_nt(k, qs[c]) for c in range(n_chains)]
            return tuple(absorb(scores[c], firsts[c], vt, k0, carries[c], False) for c in range(n_chains))

        first = skip_ref[(pl.program_id(0) * pl.num_programs(1) + pl.program_id(1)) * pl.num_programs(2) + qi]
        carries = list(lax.fori_loop(first, qi * n_chains, step, (init,) * n_chains))
        for jj in range(n_chains):
            k, vt, k0 = kv_block(qi * n_chains + jj)
            scores = {c: _dot_nt(k, qs[c]) for c in range(jj, n_chains)}
            for c in range(jj, n_chains):
                carries[c] = absorb(scores[c], firsts[c], vt, k0, carries[c], c == jj)
        for c in range(n_chains):
            _, l, acc = carries[c]
            o_ref[0, c * tqc:(c + 1) * tqc, :] = (acc / l).T.astype(BF16)


def _fox(qa, ka, vt, q_off, tq, tk, n_chains=1, skip=None):
    b, n_heads, t_q, da = qa.shape
    t_k = ka.shape[2]
    assert n_chains == 1 or (q_off == 0 and tq == n_chains * tk and t_q == t_k)
    if skip is None:
        skip = jnp.zeros((b * n_heads * (t_q // tq),), I32)
    grid_spec = pltpu.PrefetchScalarGridSpec(
        num_scalar_prefetch=1,
        grid=(b, n_heads, t_q // tq),
        in_specs=[
            pl.BlockSpec((1, 1, tq, da), lambda i, h, j, sk: (i, h, j, 0)),
            pl.BlockSpec((1, 1, t_k, da), lambda i, h, j, sk: (i, h, 0, 0)),
            pl.BlockSpec((1, 1, HEAD_DIM, t_k), lambda i, h, j, sk: (i, h, 0, 0)),
        ],
        out_specs=pl.BlockSpec((1, tq, HEAD_DIM), lambda i, h, j, sk: (i, j, h)),
    )
    return pl.pallas_call(
        functools.partial(_fox_kernel, tq=tq, tk=tk, q_off=q_off, n_kblocks=t_k // tk, n_chains=n_chains),
        grid_spec=grid_spec,
        out_shape=jax.ShapeDtypeStruct((b, t_q, n_heads * HEAD_DIM), BF16),
        compiler_params=_cparams(("arbitrary", "arbitrary", "arbitrary")),
    )(skip, qa, ka, vt)


UNDERFLOW_LOG2 = 160.0


def _fox_skip_table(sq_norms, c, n_heads, tq, tk):
    t = c.shape[1]
    bound = jnp.sqrt(sq_norms[:, 0, :n_heads] * sq_norms[:, 1, :n_heads])
    c2 = (c[:, :, :n_heads] * LOG2_E).transpose(0, 2, 1)
    c_end = c2[:, :, tk - 1::tk]
    c_q0 = c2[:, :, ::tq]
    gap = 2.0 * bound[:, :, None, None] + c_q0[:, :, :, None] - c_end[:, :, None, :]
    before = (jnp.arange(t // tk)[None, :] < (jnp.arange(t // tq) * (tq // tk))[:, None])
    dead = (gap < -UNDERFLOW_LOG2) & before[None, None]
    return jnp.sum(jnp.cumprod(dead.astype(I32), axis=-1), axis=-1).astype(I32).reshape(-1)


def _truncate_to_bf16(x):
    bits = lax.bitcast_convert_type(x, U32) & jnp.uint32(0xFFFF0000)
    return lax.bitcast_convert_type(bits, F32)


def _fox_operands(fq, fkb, fvb, c, q_rows, q_off):
    b, t_k, width = fkb.shape
    n_heads = width // HEAD_DIM
    t_q = fq.shape[1]
    c2 = (c[:, :, :n_heads] * LOG2_E).transpose(0, 2, 1)
    hi = _truncate_to_bf16(c2)
    r1 = c2 - hi
    mid = _truncate_to_bf16(r1)
    lo = r1 - mid
    lane = jnp.arange(HEAD_DIM)

    def tail(first, sign, rows):
        pick = lambda a: a[:, :, rows, None]
        terms = jnp.where(lane == first, pick(hi), jnp.where(lane == first + 1, pick(mid), pick(lo)))
        is_term = (lane >= first) & (lane < first + 3)
        return jnp.where(is_term, sign * terms, jnp.where(lane < 6, 1.0, 0.0)).astype(BF16)

    heads = lambda a: a.reshape(b, a.shape[1], n_heads, HEAD_DIM).transpose(0, 2, 1, 3)
    ka = jnp.concatenate([heads(fkb), tail(3, -1.0, slice(None))], axis=-1)
    qh = jnp.pad(heads(fq), ((0, 0), (0, 0), (0, q_rows - t_q), (0, 0)))
    qa = jnp.concatenate([qh, tail(0, 1.0, slice(q_off, q_off + q_rows))], axis=-1)
    vt = fvb.reshape(b, t_k, n_heads, HEAD_DIM).transpose(0, 2, 3, 1)
    return qa, ka, vt


def _store_row_tiled(ref, a):
    m, w = a.shape
    s = w // LANES
    for j in range(s):
        ref[pl.ds(j, m, stride=s), :] = a[:, j * LANES:(j + 1) * LANES]


def _load_row_tiled(ref, start, m, s):
    return jnp.concatenate([ref[pl.ds(start + j, m, stride=s), :] for j in range(s)], axis=1)


def _finish_kernel(x_ref, ry_ref, fo_ref, w_ref, g_ref, b_ref, h_ref, hp_ref, *, alpha, half):
    mix = _dot(ry_ref[...], w_ref[:half, :]) + _dot(fo_ref[...], w_ref[half:, :])
    h = _layer_norm(alpha * x_ref[...] + mix, g_ref[...], b_ref[...])
    h_ref[...] = h
    _store_row_tiled(hp_ref, _pack_pairs(h))


def _finish(x, ry, fo, w, g, b, alpha, tm):
    n, d = x.shape
    half = ry.shape[1]
    s = d // 2 // LANES
    row = lambda c: pl.BlockSpec((tm, c), lambda i: (i, 0))
    vec = pl.BlockSpec((1, d), lambda i: (0, 0))
    return pl.pallas_call(
        functools.partial(_finish_kernel, alpha=alpha, half=half),
        grid=(n // tm,),
        in_specs=[row(d), row(half), row(fo.shape[1]), pl.BlockSpec(w.shape, lambda i: (0, 0)), vec, vec],
        out_specs=[row(d), pl.BlockSpec((tm * s, LANES), lambda i: (i, 0))],
        out_shape=[jax.ShapeDtypeStruct((n, d), F32), jax.ShapeDtypeStruct((n * s, LANES), U32)],
        compiler_params=_cparams(("arbitrary",)),
    )(x, ry, fo, w, g, b)


def _router_kernel(h_ref, whi_ref, wlo_ref, eb_ref, c0_ref, idx_ref, gate_ref, rank_ref, cnt_ref, carry,
                   *, n_exp, tm):
    @pl.when(pl.program_id(0) == 0)
    def _():
        carry[...] = c0_ref[...].astype(F32)

    h = h_ref[...]
    hhi = h.astype(BF16)
    hlo = (h - hhi.astype(F32)).astype(BF16)
    whi = whi_ref[...]
    logits = _dot_nt(whi, hhi) + _dot_nt(whi, hlo) + _dot_nt(wlo_ref[...], hhi)
    scores = jax.nn.sigmoid(logits)
    sel = scores + eb_ref[...]
    gsz = n_exp // N_GROUPS
    eio = lax.broadcasted_iota(I32, (n_exp, tm), 0).astype(F32)
    gio = lax.broadcasted_iota(I32, (gsz, tm), 0).astype(F32)
    gs_rows = []
    for g in range(N_GROUPS):
        sg = sel[g * gsz:(g + 1) * gsz]
        m1 = jnp.max(sg, axis=0, keepdims=True)
        i1 = jnp.min(jnp.where(sg == m1, gio, float(gsz)), axis=0, keepdims=True)
        m2 = jnp.max(jnp.where(gio == i1, -jnp.inf, sg), axis=0, keepdims=True)
        gs_rows.append(m1 + m2)
    gs = jnp.concatenate(gs_rows, axis=0)
    grow = lax.broadcasted_iota(I32, (N_GROUPS, tm), 0)
    beaten = jnp.zeros((N_GROUPS, tm), F32)
    for g2 in range(N_GROUPS):
        o = gs_rows[g2]
        beats = jnp.where(o > gs, 1.0, jnp.where((o == gs) & (grow > g2), 1.0, 0.0))
        beaten = beaten + beats
    gkeep = jnp.where(beaten < float(TOPK_GROUPS), 1.0, 0.0)
    selm = jnp.concatenate(
        [jnp.where(gkeep[g:g + 1] > 0.5, sel[g * gsz:(g + 1) * gsz], -jnp.inf) for g in range(N_GROUPS)],
        axis=0)
    member = jnp.zeros((n_exp, tm), F32)
    idxs, gates = [], []
    for _ in range(TOP_K):
        m = jnp.max(selm, axis=0, keepdims=True)
        ik = jnp.min(jnp.where(selm == m, eio, float(n_exp)), axis=0, keepdims=True)
        hit = eio == ik
        gates.append(jnp.sum(jnp.where(hit, scores, 0.0), axis=0, keepdims=True))
        idxs.append(ik)
        selm = jnp.where(hit, -jnp.inf, selm)
        member = jnp.where(hit, 1.0, member)
    gsum = gates[0]
    for gk in gates[1:]:
        gsum = gsum + gk
    tr = lax.broadcasted_iota(I32, (tm, tm), 0)
    tc = lax.broadcasted_iota(I32, (tm, tm), 1)
    before = jnp.where(tr < tc, 1.0, 0.0).astype(BF16)
    prefix = _dot(member.astype(BF16), before) + carry[:, :1]
    ranks = [jnp.sum(jnp.where(eio == ik, prefix, 0.0), axis=0, keepdims=True) for ik in idxs]
    carry[...] = carry[...] + jnp.sum(member, axis=1, keepdims=True)
    idx_ref[...] = jnp.concatenate(idxs, axis=0).astype(I32)
    gate_ref[...] = jnp.concatenate([gk / gsum * ROUTED_SCALE for gk in gates], axis=0)
    rank_ref[...] = jnp.concatenate(ranks, axis=0).astype(I32)
    cnt_ref[...] = carry[...].astype(I32)


def _router(h, whi, wlo, eb, cnt0, tm):
    n, d = h.shape
    n_exp = whi.shape[0]
    tok = pl.BlockSpec((TOP_K, tm), lambda i: (0, i))
    full = lambda s: pl.BlockSpec(s, lambda i: (0, 0))
    return pl.pallas_call(
        functools.partial(_router_kernel, n_exp=n_exp, tm=tm),
        grid=(n // tm,),
        in_specs=[pl.BlockSpec((tm, d), lambda i: (i, 0)), full((n_exp, d)), full((n_exp, d)),
                  full((n_exp, 1)), full((n_exp, LANES))],
        out_specs=[tok, tok, tok, full((n_exp, LANES))],
        out_shape=[jax.ShapeDtypeStruct((TOP_K, n), I32), jax.ShapeDtypeStruct((TOP_K, n), F32),
                   jax.ShapeDtypeStruct((TOP_K, n), I32), jax.ShapeDtypeStruct((n_exp, LANES), I32)],
        scratch_shapes=[pltpu.VMEM((n_exp, LANES), F32)],
        compiler_params=_cparams(("arbitrary",)),
    )(h, whi, wlo, eb, cnt0)


def _dest_kernel(idx_ref, rank_ref, ps_ref, o_ref, *, n_exp, tm):
    eio = lax.broadcasted_iota(I32, (n_exp, tm), 0)
    ps = ps_ref[...]
    rows = []
    for k in range(TOP_K):
        hit = eio == idx_ref[k:k + 1, :]
        rows.append(jnp.sum(jnp.where(hit, ps, 0.0), axis=0, keepdims=True))
    o_ref[...] = jnp.concatenate(rows, axis=0).astype(I32) + rank_ref[...]


def _dest(idx, rank, pstart, tm):
    n = idx.shape[1]
    n_exp = pstart.shape[0]
    tok = pl.BlockSpec((TOP_K, tm), lambda i: (0, i))
    return pl.pallas_call(
        functools.partial(_dest_kernel, n_exp=n_exp, tm=tm),
        grid=(n // tm,),
        in_specs=[tok, tok, pl.BlockSpec((n_exp, 1), lambda i: (0, 0))],
        out_specs=tok,
        out_shape=jax.ShapeDtypeStruct((TOP_K, n), I32),
        compiler_params=_cparams(("arbitrary",)),
    )(idx, rank, pstart.astype(F32).reshape(n_exp, 1))


def _zero_pads_kernel(last_ref, has_ref, xs_ref, zbuf, sem):
    e = pl.program_id(0)
    n_exp = pl.num_programs(0)

    def zero_copy(start):
        return pltpu.make_async_copy(zbuf, xs_ref.at[pl.ds(pl.multiple_of(start, EXPERT_ROWS), EXPERT_ROWS)], sem)

    @pl.when(e == 0)
    def _():
        zbuf[...] = jnp.zeros_like(zbuf)

    @pl.when(has_ref[e] > 0)
    def _():
        zero_copy(last_ref[e]).start()

    @pl.when(e == n_exp - 1)
    def _():
        def drain(j, carry):
            @pl.when(has_ref[j] > 0)
            def _():
                zero_copy(0).wait()
            return carry
        lax.fori_loop(0, n_exp, drain, 0)


def _zero_pads(last_start, has_rows, n_rows, half):
    n_exp = last_start.shape[0]
    s = half // LANES
    grid_spec = pltpu.PrefetchScalarGridSpec(
        num_scalar_prefetch=2,
        grid=(n_exp,),
        in_specs=[],
        out_specs=pl.BlockSpec(memory_space=pl.ANY),
        scratch_shapes=[pltpu.VMEM((EXPERT_ROWS, s, LANES), U32), pltpu.SemaphoreType.DMA(())],
    )
    return pl.pallas_call(
        _zero_pads_kernel,
        grid_spec=grid_spec,
        out_shape=jax.ShapeDtypeStruct((n_rows, s, LANES), U32),
        compiler_params=_cparams(("arbitrary",)),
    )(last_start, has_rows)


def _dispatch_kernel(dest_ref, hp_ref, xs_in_ref, xs_ref, dsm, sem_d, sem, *, tm):
    del xs_in_ref
    i = pl.program_id(0)
    cp = pltpu.make_async_copy(dest_ref.at[i], dsm, sem_d)
    cp.start()
    cp.wait()

    def row_copy(t, d):
        return pltpu.make_async_copy(hp_ref.at[t], xs_ref.at[d], sem)

    def issue(t, carry):
        for k in range(TOP_K):
            row_copy(t, dsm[k, t]).start()
        return carry

    lax.fori_loop(0, tm, issue, 0)

    def drain(t, carry):
        for k in range(TOP_K):
            row_copy(0, 0).wait()
        return carry

    lax.fori_loop(0, tm, drain, 0)


def _dispatch(dest_tiles, hp, xs, tm):
    n, s, _ = hp.shape
    return pl.pallas_call(
        functools.partial(_dispatch_kernel, tm=tm),
        grid=(n // tm,),
        in_specs=[pl.BlockSpec(memory_space=pl.ANY),
                  pl.BlockSpec((tm, s, LANES), lambda i: (i, 0, 0)),
                  pl.BlockSpec(memory_space=pl.ANY)],
        out_specs=pl.BlockSpec(memory_space=pl.ANY),
        out_shape=jax.ShapeDtypeStruct(xs.shape, U32),
        scratch_shapes=[pltpu.SMEM((TOP_K, tm), I32), pltpu.SemaphoreType.DMA(()), pltpu.SemaphoreType.DMA(())],
        input_output_aliases={2: 0},
        compiler_params=_cparams(("arbitrary",)),
    )(dest_tiles, hp, xs)


def _experts_kernel(be_ref, nu_ref, nxt_ref, x_ref, wg_hbm, wu_hbm, wd_hbm, y_ref,
                    wg_f, wu_f, wd_f, wg_s, wu_s, wd_s, slot_ref, sem, *, half, n_exp):
    b = pl.program_id(0)
    active = b < nu_ref[0]
    e = be_ref[b]
    new_expert = (b == 0) | (e != be_ref[jnp.maximum(b - 1, 0)])

    def weight_copies(expert, s):
        return (pltpu.make_async_copy(wg_hbm.at[expert], wg_f.at[s], sem.at[s]),
                pltpu.make_async_copy(wu_hbm.at[expert], wu_f.at[s], sem.at[s]),
                pltpu.make_async_copy(wd_hbm.at[expert], wd_f.at[s], sem.at[s]))

    @pl.when(active & (b == 0))
    def _():
        slot_ref[0] = 0
        for cp in weight_copies(e, 0):
            cp.start()

    @pl.when(active & new_expert)
    def _():
        s = slot_ref[0]
        for cp in weight_copies(e, s):
            cp.wait()
        nxt = nxt_ref[e]

        @pl.when(nxt < n_exp)
        def _():
            for cp in weight_copies(nxt, 1 - s):
                cp.start()

        wg_s[...] = wg_f[s].astype(BF16)
        wu_s[...] = wu_f[s].astype(BF16)
        wd_s[...] = wd_f[s].astype(BF16)
        slot_ref[0] = 1 - s

    @pl.when(active)
    def _():
        lo, hi = _unpack_pairs(_load_row_tiled(x_ref, 0, EXPERT_ROWS, half // LANES))
        lo = lo.astype(BF16)
        hi = hi.astype(BF16)
        g = _dot(lo, wg_s[:half, :]) + _dot(hi, wg_s[half:, :])
        u = _dot(lo, wu_s[:half, :]) + _dot(hi, wu_s[half:, :])
        hm = (_silu(g) * u).astype(BF16)
        _store_row_tiled(y_ref, _pack_pairs(_dot(hm, wd_s[...])))


def _experts(block_e, n_used, next_expert, xs, wg, wu, wd):
    n_exp, d, de = wg.shape
    half = d // 2
    s = half // LANES
    n_rows = xs.shape[0] // s
    nb = n_rows // EXPERT_ROWS
    blk = lambda b, be, nu, nx: (jnp.minimum(b, nu[0] - 1), 0)
    hbm = pl.BlockSpec(memory_space=pl.ANY)
    grid_spec = pltpu.PrefetchScalarGridSpec(
        num_scalar_prefetch=3,
        grid=(nb,),
        in_specs=[pl.BlockSpec((EXPERT_ROWS * s, LANES), blk), hbm, hbm, hbm],
        out_specs=pl.BlockSpec((EXPERT_ROWS * s, LANES), blk),
        scratch_shapes=[pltpu.VMEM((2, d, de), F32), pltpu.VMEM((2, d, de), F32), pltpu.VMEM((2, de, d), F32),
                        pltpu.VMEM((d, de), BF16), pltpu.VMEM((d, de), BF16), pltpu.VMEM((de, d), BF16),
                        pltpu.SMEM((1,), I32), pltpu.SemaphoreType.DMA((2,))],
    )
    return pl.pallas_call(
        functools.partial(_experts_kernel, half=half, n_exp=n_exp),
        grid_spec=grid_spec,
        out_shape=jax.ShapeDtypeStruct((n_rows * s, LANES), U32),
        compiler_params=_cparams(("arbitrary",)),
    )(block_e, n_used, next_expert, xs, wg, wu, wd)


def _combine_kernel(dest_ref, h_ref, gate_ref, ys_ref, wsg_ref, wsu_ref, wsd_ref, g_ref, b_ref, y_ref,
                    buf, dsm, sem_d, sem, *, tm, alpha):
    i = pl.program_id(0)
    last = pl.num_programs(0) - 1
    slot = i % 2
    nslot = 1 - slot

    def table_copy(tile, s):
        return pltpu.make_async_copy(dest_ref.at[jnp.minimum(tile, last)], dsm.at[s], sem_d.at[s])

    rs = ys_ref.shape[1]

    def row_copy(s, k, t, d):
        start = (k * tm + t) * rs
        if not isinstance(start, int):
            start = pl.multiple_of(start, rs)
        return pltpu.make_async_copy(ys_ref.at[d], buf.at[s, pl.ds(start, rs)], sem.at[s])

    def wait_rows(s):
        def drain(t, carry):
            for k in range(TOP_K):
                row_copy(s, 0, 0, 0).wait()
            return carry
        lax.fori_loop(0, tm, drain, 0)

    @pl.when(i == 0)
    def _():
        table_copy(0, 0).start()
        table_copy(0, 0).wait()

        def issue(t, carry):
            for k in range(TOP_K):
                row_copy(0, k, t, dsm[0, k, t]).start()
            return carry
        lax.fori_loop(0, tm, issue, 0)
        table_copy(1, 1).start()

    table_copy(i + 1, nslot).wait()
    wait_rows(slot)
    table_copy(i + 2, slot).start()

    for t in range(tm):
        for k in range(TOP_K):
            row_copy(nslot, k, t, dsm[nslot, k, t]).start()

    h = h_ref[...]
    hb = h.astype(BF16)
    sh = _dot((_silu(_dot(hb, wsg_ref[...])) * _dot(hb, wsu_ref[...])).astype(BF16), wsd_ref[...])

    gate = gate_ref[...]
    rows = buf.at[slot]
    acc_lo = jnp.zeros((tm, rs * LANES), F32)
    acc_hi = jnp.zeros((tm, rs * LANES), F32)
    for k in range(TOP_K):
        lo, hi = _unpack_pairs(_load_row_tiled(rows, k * tm * rs, tm, rs))
        gk = gate[:, k:k + 1]
        acc_lo = acc_lo + gk * lo
        acc_hi = acc_hi + gk * hi
    routed = jnp.concatenate([acc_lo, acc_hi], axis=1)
    y_ref[...] = _layer_norm(alpha * h + (routed + sh), g_ref[...], b_ref[...])

    @pl.when(i == last)
    def _():
        wait_rows(nslot)
        table_copy(i + 2, slot).wait()


def _combine(dest_tiles, h, gate_t, ys, wsg, wsu, wsd, g, b, alpha, tm):
    n, d = h.shape
    rs = ys.shape[1]
    full = lambda a: pl.BlockSpec(a.shape, lambda i: (0, 0))
    return pl.pallas_call(
        functools.partial(_combine_kernel, tm=tm, alpha=alpha),
        grid=(n // tm,),
        in_specs=[pl.BlockSpec(memory_space=pl.ANY),
                  pl.BlockSpec((tm, d), lambda i: (i, 0)),
                  pl.BlockSpec((tm, TOP_K), lambda i: (i, 0)),
                  pl.BlockSpec(memory_space=pl.ANY),
                  full(wsg), full(wsu), full(wsd), full(g), full(b)],
        out_specs=pl.BlockSpec((tm, d), lambda i: (i, 0)),
        out_shape=jax.ShapeDtypeStruct((n, d), F32),
        scratch_shapes=[pltpu.VMEM((2, TOP_K * tm * rs, LANES), U32), pltpu.SMEM((2, TOP_K, tm), I32),
                        pltpu.SemaphoreType.DMA((2,)), pltpu.SemaphoreType.DMA((2,))],
        compiler_params=_cparams(("arbitrary",)),
    )(dest_tiles, h, gate_t, ys, wsg, wsu, wsd, g, b)


def _rope_tables(pos):
    half = HEAD_DIM // 2
    inv_freq = ROPE_BASE ** (-jnp.arange(half, dtype=F32) / half)
    ang = pos.astype(F32)[:, None] * inv_freq[None, :]
    cos, sin = jnp.cos(ang), jnp.sin(ang)
    return jnp.concatenate([cos, cos], axis=-1), jnp.concatenate([-sin, sin], axis=-1)


def _dest_tiles(dest, tm):
    k, n = dest.shape
    return dest.reshape(k, n // tm, tm).transpose(1, 0, 2)


def _layer(xp, xs, state_ret, cache_k, cache_v, cache_logf, w_in, b_fgate, ret_gn_g, w_out,
           ln1_g, ln1_b, w_router, e_bias, w_e_gate, w_e_up, w_e_down, w_s_gate, w_s_up, w_s_down,
           ln2_g, ln2_b, alpha):
    bp, tp, d = xp.shape
    bs, ts, _ = xs.shape
    past = cache_k.shape[1]
    width = w_out.shape[0] // 2
    n_heads = width // HEAD_DIM
    n_exp = w_router.shape[1]

    w_ret = w_in[:, :4 * width].astype(BF16)
    w_fox = w_in[:, 4 * width:7 * width].astype(BF16)
    n_f = w_in.shape[1] - 7 * width
    w_f = jnp.pad(w_in[:, 7 * width:], ((0, 0), (0, LANES - n_f))).astype(BF16)
    b_f = jnp.pad(b_fgate, (0, LANES - n_f)).reshape(1, LANES)
    gn = ret_gn_g.reshape(1, width)
    w_o = w_out.astype(BF16)
    l1g, l1b = ln1_g.reshape(1, d), ln1_b.reshape(1, d)
    l2g, l2b = ln2_g.reshape(1, d), ln2_b.reshape(1, d)
    wr_t = w_router.T
    wr_top = _truncate_to_bf16(wr_t)
    wr_hi = wr_top.astype(BF16)
    wr_lo = (wr_t - wr_top).astype(BF16)
    eb = e_bias.reshape(n_exp, 1)
    wsg, wsu, wsd = w_s_gate.astype(BF16), w_s_up.astype(BF16), w_s_down.astype(BF16)

    xp2 = xp.reshape(bp * tp, d)
    tm_p = _tile(tp, 512)
    cs_p, sn_p = _rope_tables(jnp.arange(tp))
    pr = _proj_ret(xp2, w_ret, cs_p, sn_p, tp, tm_p).reshape(bp, tp, 4 * width)
    fk, fv, lf, c_p, qa_p, ka_p, vt_p, nrm_p = _proj_fox_heads(xp2, w_fox, w_f, b_f, bp, _tile(tp, 256))
    s0 = jnp.zeros((bp, n_heads, HEAD_DIM, HEAD_DIM), F32)
    ry_p, sfin_p = _retention(pr, s0, gn, _tile(tp, 256))
    lf_p = lf.reshape(bp, tp, LANES)
    tq_p = _tile(tp, 2048)
    skip_p = _fox_skip_table(nrm_p, c_p.reshape(bp, tp, LANES), n_heads, tq_p, tq_p // 2)
    fo_p = _fox(qa_p, ka_p, vt_p, 0, tq_p, tq_p // 2, n_chains=2, skip=skip_p)
    h_p, hp_p = _finish(xp2, ry_p.reshape(bp * tp, width), fo_p.reshape(bp * tp, width), w_o, l1g, l1b,
                        alpha, _tile(bp * tp, 256))

    xs2 = xs.reshape(bs * ts, d)
    tm_s = _tile(ts, 512)
    cs_s, sn_s = _rope_tables(past + jnp.arange(ts))
    prs = _proj_ret(xs2, w_ret, cs_s, sn_s, ts, tm_s).reshape(bs, ts, 4 * width)
    fq_s, fk_s, fv_s, fkb_s, fvb_s, lf_sn = _proj_fox(xs2, w_fox, w_f, b_f, tm_s)
    ry_s, sfin_s = _retention(prs, state_ret.astype(F32), gn, ts)
    tk_s = LANES
    tq_s = -(-ts // LANES) * LANES
    t_all = -(-(past + tq_s) // tk_s) * tk_s
    pad_t = t_all - past - ts
    lf_s3 = lf_sn.reshape(bs, ts, LANES)
    lf_all = jnp.concatenate([
        jnp.pad(cache_logf.astype(F32), ((0, 0), (0, 0), (0, LANES - n_heads))),
        lf_s3, jnp.zeros((bs, pad_t, LANES), F32)], axis=1)
    c_s = _cumsum(lf_all, _tile(t_all, 1088))
    zpad = jnp.zeros((bs, pad_t, width), BF16)
    k_all = jnp.concatenate([cache_k.reshape(bs, past, width).astype(BF16),
                             fkb_s.reshape(bs, ts, width), zpad], axis=1)
    v_all = jnp.concatenate([cache_v.reshape(bs, past, width).astype(BF16),
                             fvb_s.reshape(bs, ts, width), zpad], axis=1)
    qa_s, ka_s, vt_s = _fox_operands(fq_s.reshape(bs, ts, width), k_all, v_all, c_s, tq_s, past)
    fo_s = _fox(qa_s, ka_s, vt_s, past, tq_s, t_all)[:, :ts]
    h_s, hp_s = _finish(xs2, ry_s.reshape(bs * ts, width), fo_s.reshape(bs * ts, width), w_o, l1g, l1b,
                        alpha, _tile(bs * ts, 256))

    n_p, n_s = bp * tp, bs * ts
    tm_r = 256
    cnt0 = jnp.zeros((n_exp, LANES), I32)
    idx_p, gate_p, rank_p, cnt1 = _router(h_p, wr_hi, wr_lo, eb, cnt0, _tile(n_p, tm_r))
    idx_s, gate_s, rank_s, cnt2 = _router(h_s, wr_hi, wr_lo, eb, cnt1, _tile(n_s, tm_r))
    counts = cnt2[:, 0]
    padded = (counts + EXPERT_ROWS - 1) // EXPERT_ROWS * EXPERT_ROWS
    pend = jnp.cumsum(padded)
    pstart = pend - padded
    dest_p = _dest(idx_p, rank_p, pstart, _tile(n_p, 512))
    dest_s = _dest(idx_s, rank_s, pstart, _tile(n_s, 512))
    n_blocks = -(-((n_p + n_s) * TOP_K) // EXPERT_ROWS) + n_exp
    n_used = (pend[-1] // EXPERT_ROWS).astype(I32).reshape(1)
    block_start = jnp.arange(n_blocks, dtype=I32) * EXPERT_ROWS
    block_e = jnp.minimum(jnp.sum((pend[None, :] <= block_start[:, None]).astype(I32), axis=1),
                          n_exp - 1).astype(I32)
    tm_d = 256
    dt_p = _dest_tiles(dest_p, _tile(n_p, tm_d))
    dt_s = _dest_tiles(dest_s, _tile(n_s, tm_d))
    rs = d // 2 // LANES
    n_rows = n_blocks * EXPERT_ROWS
    xs_rows = _zero_pads((pend - EXPERT_ROWS).astype(I32), padded.astype(I32), n_rows, d // 2)
    xs_rows = _dispatch(dt_p, hp_p.reshape(n_p, rs, LANES), xs_rows, _tile(n_p, tm_d))
    xs_rows = _dispatch(dt_s, hp_s.reshape(n_s, rs, LANES), xs_rows, _tile(n_s, tm_d))
    xs_rows = xs_rows.reshape(n_rows * rs, LANES)
    owners = jnp.where(counts > 0, jnp.arange(n_exp, dtype=I32), n_exp)
    later = lax.cummin(owners[::-1])[::-1]
    next_expert = jnp.concatenate([later[1:], jnp.full((1,), n_exp, I32)]).astype(I32)
    ys_rows = _experts(block_e, n_used, next_expert, xs_rows, w_e_gate, w_e_up, w_e_down)
    ys_rows = ys_rows.reshape(n_rows, rs, LANES)
    tm_c = 128
    y_p = _combine(_dest_tiles(dest_p, _tile(n_p, tm_c)), h_p, gate_p.T, ys_rows, wsg, wsu, wsd, l2g, l2b,
                   alpha, _tile(n_p, tm_c))
    y_s = _combine(_dest_tiles(dest_s, _tile(n_s, tm_c)), h_s, gate_s.T, ys_rows, wsg, wsu, wsd, l2g, l2b,
                   alpha, _tile(n_s, tm_c))

    outs_p = (sfin_p, fk.reshape(bp, tp, n_heads, HEAD_DIM), fv.reshape(bp, tp, n_heads, HEAD_DIM),
              lf_p[:, :, :n_heads])
    outs_s = (sfin_s, fk_s.reshape(bs, ts, n_heads, HEAD_DIM), fv_s.reshape(bs, ts, n_heads, HEAD_DIM),
              lf_s3[:, :, :n_heads])
    return y_p.reshape(bp, tp, d), y_s.reshape(bs, ts, d), outs_p, outs_s


def kernel(x_prompt, x_sample, state_ret, cache_fox_k, cache_fox_v, cache_fox_logf, w_in, b_fgate, ret_gn_g, w_out, ln1_g, ln1_b, w_router, e_bias, w_e_gate, w_e_up, w_e_down, w_s_gate, w_s_up, w_s_down, ln2_g, ln2_b):
    depth = w_in.shape[0]
    alpha = (2.0 * depth) ** 0.25
    xp, xs = x_prompt, x_sample
    per_p, per_s = [], []
    for l in range(depth):
        xp, xs, op, os_ = _layer(
            xp, xs, state_ret[l], cache_fox_k[l], cache_fox_v[l], cache_fox_logf[l], w_in[l], b_fgate[l],
            ret_gn_g[l], w_out[l], ln1_g[l], ln1_b[l], w_router[l], e_bias[l], w_e_gate[l], w_e_up[l],
            w_e_down[l], w_s_gate[l], w_s_up[l], w_s_down[l], ln2_g[l], ln2_b[l], alpha)
        per_p.append(op)
        per_s.append(os_)
    stack = lambda items, j: jnp.stack([it[j] for it in items])
    return (xp, xs,
            stack(per_p, 0), stack(per_p, 1), stack(per_p, 2), stack(per_p, 3),
            stack(per_s, 0).astype(state_ret.dtype), stack(per_s, 1), stack(per_s, 2),
            stack(per_s, 3).astype(cache_fox_logf.dtype))
```

```python
import functools
import math

import jax
import jax.numpy as jnp
from jax import lax
from jax.experimental import pallas as pl
from jax.experimental.pallas import tpu as pltpu

HEAD_DIM = 128
ROPE_BASE = 10000.0
N_GROUPS = 8
TOPK_GROUPS = 4
TOP_K = 8
ROUTED_SCALE = 2.5
LN_EPS = 1e-5
LANES = 128
EXPERT_ROWS = 256
LOG2_E = math.log2(math.e)
VMEM_LIMIT_BYTES = 56 * 1024 * 1024

F32 = jnp.float32
BF16 = jnp.bfloat16
U32 = jnp.uint32
I32 = jnp.int32


def _cparams(sem):
    return pltpu.CompilerParams(dimension_semantics=sem, vmem_limit_bytes=VMEM_LIMIT_BYTES)


def _tile(n, pref):
    t = min(n, pref)
    while n % t:
        t -= 1
    return t


def _dot(a, b):
    return jnp.dot(a, b, preferred_element_type=F32)


def _dot_nt(a, b):
    return lax.dot_general(a, b, (((1,), (1,)), ((), ())), preferred_element_type=F32)


def _dot_tn(a, b):
    return lax.dot_general(a, b, (((0,), (0,)), ((), ())), preferred_element_type=F32)


def _pack_pairs(a):
    n = a.shape[1] // 2
    lo = lax.bitcast_convert_type(a[:, :n].astype(BF16).astype(F32), U32)
    hi = lax.bitcast_convert_type(a[:, n:].astype(BF16).astype(F32), U32)
    return (lo >> 16) | (hi & jnp.uint32(0xFFFF0000))


def _unpack_pairs(u):
    lo = lax.bitcast_convert_type(u << 16, F32)
    hi = lax.bitcast_convert_type(u & jnp.uint32(0xFFFF0000), F32)
    return lo, hi


def _layer_norm(z, g, b):
    mu = jnp.mean(z, axis=-1, keepdims=True)
    zc = z - mu
    var = jnp.mean(zc * zc, axis=-1, keepdims=True)
    return zc * lax.rsqrt(var + LN_EPS) * g + b


def _silu(g):
    return g * jax.nn.sigmoid(g)


def _proj_ret_kernel(x_ref, w_ref, cs_ref, sn_ref, o_ref, *, width, scale):
    xb = x_ref[...].astype(BF16)
    cs = cs_ref[...]
    sn = sn_ref[...]
    for sec in range(4):
        p = _dot(xb, w_ref[:, sec * width:(sec + 1) * width])
        if sec < 2:
            for h in range(width // HEAD_DIM):
                ph = p[:, h * HEAD_DIM:(h + 1) * HEAD_DIM]
                r = ph * cs + pltpu.roll(ph, HEAD_DIM // 2, 1) * sn
                if sec == 1:
                    r = r * scale
                o_ref[:, sec * width + h * HEAD_DIM:sec * width + (h + 1) * HEAD_DIM] = r.astype(BF16)
        else:
            o_ref[:, sec * width:(sec + 1) * width] = p.astype(BF16)


def _proj_ret(x, w, cs, sn, seq, tm):
    n, d = x.shape
    width = w.shape[1] // 4
    nt_seq = seq // tm
    return pl.pallas_call(
        functools.partial(_proj_ret_kernel, width=width, scale=HEAD_DIM ** -0.5),
        grid=(n // tm,),
        in_specs=[
            pl.BlockSpec((tm, d), lambda i: (i, 0)),
            pl.BlockSpec((d, 4 * width), lambda i: (0, 0)),
            pl.BlockSpec((tm, HEAD_DIM), lambda i: (i % nt_seq, 0)),
            pl.BlockSpec((tm, HEAD_DIM), lambda i: (i % nt_seq, 0)),
        ],
        out_specs=pl.BlockSpec((tm, 4 * width), lambda i: (i, 0)),
        out_shape=jax.ShapeDtypeStruct((n, 4 * width), BF16),
        compiler_params=_cparams(("arbitrary",)),
    )(x, w, cs, sn)


def _proj_fox_kernel(x_ref, w_ref, wf_ref, bf_ref, q_ref, k_ref, v_ref, kb_ref, vb_ref, lf_ref,
                     *, width, scale):
    xb = x_ref[...].astype(BF16)
    q = _dot(xb, w_ref[:, :width])
    q_ref[...] = (q * scale).astype(BF16)
    k = _dot(xb, w_ref[:, width:2 * width])
    k_ref[...] = k
    kb_ref[...] = k.astype(BF16)
    v = _dot(xb, w_ref[:, 2 * width:])
    v_ref[...] = v
    vb_ref[...] = v.astype(BF16)
    z = _dot(xb, wf_ref[...]) + bf_ref[...]
    lf_ref[...] = jnp.minimum(z, 0.0) - jnp.log1p(jnp.exp(-jnp.abs(z)))


def _log_sigmoid(z):
    return jnp.minimum(z, 0.0) - jnp.log1p(jnp.exp(-jnp.abs(z)))


def _proj_fox_heads_kernel(x_ref, w_ref, wf_ref, bf_ref, k_ref, v_ref, lf_ref, c_ref, qa_ref, ka_ref, vt_ref,
                           nrm_ref, carry, *, width, scale, n_heads):
    @pl.when(pl.program_id(1) == 0)
    def _():
        carry[...] = jnp.zeros_like(carry)

    tm = x_ref.shape[0]
    xb = x_ref[...].astype(BF16)
    q = _dot(xb, w_ref[:, :width]) * scale
    k = _dot(xb, w_ref[:, width:2 * width])
    v = _dot(xb, w_ref[:, 2 * width:])
    k_ref[...] = k
    v_ref[...] = v
    logf = _log_sigmoid(_dot(xb, wf_ref[...]) + bf_ref[...])
    lf_ref[...] = logf
    c = _tile_cumsum(logf, carry[...])
    c_ref[...] = c
    carry[...] = c[tm - 1:tm, :]
    c2 = c * LOG2_E
    lane = lax.broadcasted_iota(I32, (tm, HEAD_DIM), 1)
    lane_row = lax.broadcasted_iota(I32, (1, LANES), 1)

    def max_sq_norm(a):
        af = a.astype(F32)
        return jnp.max(jnp.sum(af * af, axis=-1, keepdims=True), axis=0, keepdims=True)

    qn = jnp.zeros((1, LANES), F32)
    kn = jnp.zeros((1, LANES), F32)
    for h in range(n_heads):
        sl = slice(h * HEAD_DIM, (h + 1) * HEAD_DIM)
        hi, mid, lo = _split3(c2[:, h:h + 1])
        terms = jnp.where(lane == 0, hi.astype(F32), jnp.where(lane == 1, mid.astype(F32), lo.astype(F32)))
        q_tail = jnp.where(lane < 3, terms, jnp.where(lane < 6, 1.0, 0.0))
        k_tail = jnp.where(lane < 3, 1.0, jnp.where(lane < 6, -pltpu.roll(terms, 3, 1), 0.0))
        qh = q[:, sl].astype(BF16)
        kh = k[:, sl].astype(BF16)
        qa_ref[0, h, :, :HEAD_DIM] = qh
        qa_ref[0, h, :, HEAD_DIM:] = q_tail.astype(BF16)
        ka_ref[0, h, :, :HEAD_DIM] = kh
        ka_ref[0, h, :, HEAD_DIM:] = k_tail.astype(BF16)
        vt_ref[0, h] = v[:, sl].T.astype(BF16)
        qn = jnp.where(lane_row == h, max_sq_norm(qh), qn)
        kn = jnp.where(lane_row == h, max_sq_norm(kh), kn)

    @pl.when(pl.program_id(1) == 0)
    def _():
        nrm_ref[...] = jnp.zeros_like(nrm_ref)

    nrm_ref[0, 0:1, :] = jnp.maximum(nrm_ref[0, 0:1, :], qn)
    nrm_ref[0, 1:2, :] = jnp.maximum(nrm_ref[0, 1:2, :], kn)


def _proj_fox_heads(x, w, wf, bfg, batch, tm):
    n, d = x.shape
    width = w.shape[1] // 3
    n_heads = width // HEAD_DIM
    seq = n // batch
    nt = seq // tm
    row = lambda c: pl.BlockSpec((tm, c), lambda b, i: (b * nt + i, 0))
    const = lambda s: pl.BlockSpec(s, lambda b, i: (0, 0))
    aug = pl.BlockSpec((1, n_heads, tm, 2 * HEAD_DIM), lambda b, i: (b, 0, i, 0))
    return pl.pallas_call(
        functools.partial(_proj_fox_heads_kernel, width=width, scale=HEAD_DIM ** -0.5 * LOG2_E,
                          n_heads=n_heads),
        grid=(batch, nt),
        in_specs=[row(d), const((d, 3 * width)), const((d, LANES)), const((1, LANES))],
        out_specs=[row(width), row(width), row(LANES), row(LANES), aug, aug,
                   pl.BlockSpec((1, n_heads, HEAD_DIM, tm), lambda b, i: (b, 0, 0, i)),
                   pl.BlockSpec((1, 8, LANES), lambda b, i: (b, 0, 0))],
        out_shape=[
            jax.ShapeDtypeStruct((n, width), F32),
            jax.ShapeDtypeStruct((n, width), F32),
            jax.ShapeDtypeStruct((n, LANES), F32),
            jax.ShapeDtypeStruct((n, LANES), F32),
            jax.ShapeDtypeStruct((batch, n_heads, seq, 2 * HEAD_DIM), BF16),
            jax.ShapeDtypeStruct((batch, n_heads, seq, 2 * HEAD_DIM), BF16),
            jax.ShapeDtypeStruct((batch, n_heads, HEAD_DIM, seq), BF16),
            jax.ShapeDtypeStruct((batch, 8, LANES), F32),
        ],
        scratch_shapes=[pltpu.VMEM((1, LANES), F32)],
        compiler_params=_cparams(("arbitrary", "arbitrary")),
    )(x, w, wf, bfg)


def _proj_fox(x, w, wf, bfg, tm):
    n, d = x.shape
    width = w.shape[1] // 3
    row = lambda c: pl.BlockSpec((tm, c), lambda i: (i, 0))
    return pl.pallas_call(
        functools.partial(_proj_fox_kernel, width=width, scale=HEAD_DIM ** -0.5 * LOG2_E),
        grid=(n // tm,),
        in_specs=[
            row(d),
            pl.BlockSpec((d, 3 * width), lambda i: (0, 0)),
            pl.BlockSpec((d, LANES), lambda i: (0, 0)),
            pl.BlockSpec((1, LANES), lambda i: (0, 0)),
        ],
        out_specs=[row(width), row(width), row(width), row(width), row(width), row(LANES)],
        out_shape=[
            jax.ShapeDtypeStruct((n, width), BF16),
            jax.ShapeDtypeStruct((n, width), F32),
            jax.ShapeDtypeStruct((n, width), F32),
            jax.ShapeDtypeStruct((n, width), BF16),
            jax.ShapeDtypeStruct((n, width), BF16),
            jax.ShapeDtypeStruct((n, LANES), F32),
        ],
        compiler_params=_cparams(("arbitrary",)),
    )(x, w, wf, bfg)


def _split3(x):
    hi = x.astype(BF16)
    r1 = x - hi.astype(F32)
    mid = r1.astype(BF16)
    lo = (r1 - mid.astype(F32)).astype(BF16)
    return hi, mid, lo


def _tile_cumsum(x, carry_row):
    tm = x.shape[0]
    r = lax.broadcasted_iota(I32, (tm, tm), 0)
    c = lax.broadcasted_iota(I32, (tm, tm), 1)
    tri = jnp.where(c <= r, 1.0, 0.0).astype(BF16)
    hi, mid, lo = _split3(x)
    return _dot(tri, hi) + _dot(tri, mid) + _dot(tri, lo) + carry_row


def _cumsum_kernel(x_ref, o_ref, carry, *, tm):
    @pl.when(pl.program_id(1) == 0)
    def _():
        carry[...] = jnp.zeros_like(carry)

    out = _tile_cumsum(x_ref[0], carry[...])
    o_ref[0] = out
    carry[...] = out[tm - 1:tm, :]


def _cumsum(x, tm):
    b, t, _ = x.shape
    return pl.pallas_call(
        functools.partial(_cumsum_kernel, tm=tm),
        grid=(b, t // tm),
        in_specs=[pl.BlockSpec((1, tm, LANES), lambda i, j: (i, j, 0))],
        out_specs=pl.BlockSpec((1, tm, LANES), lambda i, j: (i, j, 0)),
        out_shape=jax.ShapeDtypeStruct(x.shape, F32),
        scratch_shapes=[pltpu.VMEM((1, LANES), F32)],
        compiler_params=_cparams(("arbitrary", "arbitrary")),
    )(x)


def _retention_kernel(q_ref, k_ref, v_ref, g_ref, s0_ref, gn_ref, y_ref, sout_ref, s_scr, decay_scr,
                      *, n_heads, chunk):
    c = pl.program_id(1)

    @pl.when(c == 0)
    def _():
        s_scr[...] = s0_ref[0]

    @pl.when((pl.program_id(0) == 0) & (c == 0))
    def _():
        row = lax.broadcasted_iota(I32, (chunk, chunk), 0)
        col = lax.broadcasted_iota(I32, (chunk, chunk), 1)
        rel = (row - col).astype(F32)
        for h in range(n_heads):
            lg = math.log1p(-(2.0 ** (-5 - h)))
            decay_scr[h] = jnp.where(rel >= 0, jnp.exp(lg * jnp.maximum(rel, 0.0)), 0.0)

    ri = lax.broadcasted_iota(I32, (chunk, HEAD_DIM), 0).astype(F32)
    for h in range(n_heads):
        sl = slice(h * HEAD_DIM, (h + 1) * HEAD_DIM)
        lg = math.log1p(-(2.0 ** (-5 - h)))
        decay = decay_scr[h]
        q = q_ref[0, :, sl]
        k = k_ref[0, :, sl]
        v = v_ref[0, :, sl]
        state = s_scr[h]
        scores = _dot_nt(q, k) * decay
        o = _dot(scores.astype(BF16), v)
        o = o + jnp.exp(lg * (ri + 1.0)) * _dot(q, state.astype(BF16))
        kd = (k.astype(F32) * jnp.exp(lg * (chunk - 1.0 - ri))).astype(BF16)
        s_scr[h] = math.exp(lg * chunk) * state + _dot_tn(kd, v)
        mu = jnp.mean(o, axis=-1, keepdims=True)
        oc = o - mu
        var = jnp.mean(oc * oc, axis=-1, keepdims=True)
        yn = oc * lax.rsqrt(var + LN_EPS) * gn_ref[:, sl]
        y_ref[0, :, sl] = (_silu(g_ref[0, :, sl].astype(F32)) * yn).astype(BF16)

    @pl.when(c == pl.num_programs(1) - 1)
    def _():
        sout_ref[0] = s_scr[...]


def _retention(p, s0, gn, chunk):
    b, t, w4 = p.shape
    width = w4 // 4
    n_heads = width // HEAD_DIM
    sec = lambda s: pl.BlockSpec((1, chunk, width), lambda i, j: (i, j, s))
    st = pl.BlockSpec((1, n_heads, HEAD_DIM, HEAD_DIM), lambda i, j: (i, 0, 0, 0))
    return pl.pallas_call(
        functools.partial(_retention_kernel, n_heads=n_heads, chunk=chunk),
        grid=(b, t // chunk),
        in_specs=[sec(0), sec(1), sec(2), sec(3), st, pl.BlockSpec((1, width), lambda i, j: (0, 0))],
        out_specs=[pl.BlockSpec((1, chunk, width), lambda i, j: (i, j, 0)), st],
        out_shape=[jax.ShapeDtypeStruct((b, t, width), BF16),
                   jax.ShapeDtypeStruct(s0.shape, F32)],
        scratch_shapes=[pltpu.VMEM((n_heads, HEAD_DIM, HEAD_DIM), F32), pltpu.VMEM((n_heads, chunk, chunk), F32)],
        compiler_params=_cparams(("arbitrary", "arbitrary")),
    )(p, p, p, p, s0, gn)


def _fox_kernel(skip_ref, q_ref, k_ref, vt_ref, o_ref, *, tq, tk, q_off, n_kblocks, n_chains):
    qi = pl.program_id(2)
    tqc = tq // n_chains

    def absorb(s, q_first, vt, k0, carry, masked):
        m, l, acc = carry
        if masked:
            kpos = k0 + lax.broadcasted_iota(I32, (tk, tqc), 0)
            qpos = q_first + lax.broadcasted_iota(I32, (tk, tqc), 1)
            s = jnp.where(kpos <= qpos, s, -jnp.inf)
        m_new = jnp.maximum(m, jnp.max(s, axis=0, keepdims=True))
        alpha = jnp.exp2(m - m_new)
        p = jnp.exp2(s - m_new)
        l = alpha * l + jnp.sum(p, axis=0, keepdims=True)
        acc = alpha * acc + _dot(vt, p.astype(BF16))
        return m_new, l, acc

    def kv_block(j):
        k0 = pl.multiple_of(j * tk, tk)
        return k_ref[0, 0, pl.ds(k0, tk), :], vt_ref[0, 0, :, pl.ds(k0, tk)], k0

    init =(jnp.full((1, tqc), -jnp.inf, F32), jnp.zeros((1, tqc), F32), jnp.zeros((HEAD_DIM, tqc), F32))

    if n_chains == 1:
        q = q_ref[0, 0]
        q_first = q_off + qi * tq
        n_full = jnp.minimum((q_first + 1) // tk, n_kblocks)
        n_tot = jnp.minimum((q_first + tq + tk - 1) // tk, n_kblocks)

        def step(j, carry, masked):
            k, vt, k0 = kv_block(j)
            return absorb(_dot_nt(k, q), q_first, vt, k0, carry, masked)

        carry = lax.fori_loop(0, n_full, functools.partial(step, masked=False), init)
        _, l, acc = lax.fori_loop(n_full, n_tot, functools.partial(step, masked=True), carry)
        o_ref[0] = (acc / l).T.astype(BF16)
    else:
        qs = [q_ref[0, 0, c * tqc:(c + 1) * tqc, :] for c in range(n_chains)]
        firsts = [qi * tq + c * tqc for c in range(n_chains)]

        def step(j, carries):
            k, vt, k0 = kv_block(j)
            scores = [_dot_nt(k, qs[c]) for c in range(n_chains)]
            return tuple(absorb(scores[c], firsts[c], vt, k0, carries[c], False) for c in range(n_chains))

        first = skip_ref[(pl.program_id(0) * pl.num_programs(1) + pl.program_id(1)) * pl.num_programs(2) + qi]
        carries = list(lax.fori_loop(first, qi * n_chains, step, (init,) * n_chains))
        for jj in range(n_chains):
            k, vt, k0 = kv_block(qi * n_chains + jj)
            scores = {c: _dot_nt(k, qs[c]) for c in range(jj, n_chains)}
            for c in range(jj, n_chains):
                carries[c] = absorb(scores[c], firsts[c], vt, k0, carries[c], c == jj)
        for c in range(n_chains):
            _, l, acc = carries[c]
            o_ref[0, c * tqc:(c + 1) * tqc, :] = (acc / l).T.astype(BF16)


def _fox(qa, ka, vt, q_off, tq, tk, n_chains=1, skip=None):
    b, n_heads, t_q, da = qa.shape
    t_k = ka.shape[2]
    assert n_chains == 1 or (q_off == 0 and tq == n_chains * tk and t_q == t_k)
    if skip is None:
        skip = jnp.zeros((b * n_heads * (t_q // tq),), I32)
    grid_spec = pltpu.PrefetchScalarGridSpec(
        num_scalar_prefetch=1,
        grid=(b, n_heads, t_q // tq),
        in_specs=[
            pl.BlockSpec((1, 1, tq, da), lambda i, h, j, sk: (i, h, j, 0)),
            pl.BlockSpec((1, 1, t_k, da), lambda i, h, j, sk: (i, h, 0, 0)),
            pl.BlockSpec((1, 1, HEAD_DIM, t_k), lambda i, h, j, sk: (i, h, 0, 0)),
        ],
        out_specs=pl.BlockSpec((1, tq, HEAD_DIM), lambda i, h, j, sk: (i, j, h)),
    )
    return pl.pallas_call(
        functools.partial(_fox_kernel, tq=tq, tk=tk, q_off=q_off, n_kblocks=t_k // tk, n_chains=n_chains),
        grid_spec=grid_spec,
        out_shape=jax.ShapeDtypeStruct((b, t_q, n_heads * HEAD_DIM), BF16),
        compiler_params=_cparams(("arbitrary", "arbitrary", "arbitrary")),
    )(skip, qa, ka, vt)


UNDERFLOW_LOG2 = 160.0


def _fox_skip_table(sq_norms, c, n_heads, tq, tk):
    t = c.shape[1]
    bound = jnp.sqrt(sq_norms[:, 0, :n_heads] * sq_norms[:, 1, :n_heads])
    c2 = (c[:, :, :n_heads] * LOG2_E).transpose(0, 2, 1)
    c_end = c2[:, :, tk - 1::tk]
    c_q0 = c2[:, :, ::tq]
    gap = 2.0 * bound[:, :, None, None] + c_q0[:, :, :, None] - c_end[:, :, None, :]
    before = (jnp.arange(t // tk)[None, :] < (jnp.arange(t // tq) * (tq // tk))[:, None])
    dead = (gap < -UNDERFLOW_LOG2) & before[None, None]
    return jnp.sum(jnp.cumprod(dead.astype(I32), axis=-1), axis=-1).astype(I32).reshape(-1)


def _truncate_to_bf16(x):
    bits = lax.bitcast_convert_type(x, U32) & jnp.uint32(0xFFFF0000)
    return lax.bitcast_convert_type(bits, F32)


def _fox_operands(fq, fkb, fvb, c, q_rows, q_off):
    b, t_k, width = fkb.shape
    n_heads = width // HEAD_DIM
    t_q = fq.shape[1]
    c2 = (c[:, :, :n_heads] * LOG2_E).transpose(0, 2, 1)
    hi = _truncate_to_bf16(c2)
    r1 = c2 - hi
    mid = _truncate_to_bf16(r1)
    lo = r1 - mid
    lane = jnp.arange(HEAD_DIM)

    def tail(first, sign, rows):
        pick = lambda a: a[:, :, rows, None]
        terms = jnp.where(lane == first, pick(hi), jnp.where(lane == first + 1, pick(mid), pick(lo)))
        is_term = (lane >= first) & (lane < first + 3)
        return jnp.where(is_term, sign * terms, jnp.where(lane < 6, 1.0, 0.0)).astype(BF16)

    heads = lambda a: a.reshape(b, a.shape[1], n_heads, HEAD_DIM).transpose(0, 2, 1, 3)
    ka = jnp.concatenate([heads(fkb), tail(3, -1.0, slice(None))], axis=-1)
    qh = jnp.pad(heads(fq), ((0, 0), (0, 0), (0, q_rows - t_q), (0, 0)))
    qa = jnp.concatenate([qh, tail(0, 1.0, slice(q_off, q_off + q_rows))], axis=-1)
    vt = fvb.reshape(b, t_k, n_heads, HEAD_DIM).transpose(0, 2, 3, 1)
    return qa, ka, vt


def _store_row_tiled(ref, a):
    m, w = a.shape
    s = w // LANES
    for j in range(s):
        ref[pl.ds(j, m, stride=s), :] = a[:, j * LANES:(j + 1) * LANES]


def _load_row_tiled(ref, start, m, s):
    return jnp.concatenate([ref[pl.ds(start + j, m, stride=s), :] for j in range(s)], axis=1)


def _finish_kernel(x_ref, ry_ref, fo_ref, w_ref, g_ref, b_ref, h_ref, hp_ref, *, alpha, half):
    mix = _dot(ry_ref[...], w_ref[:half, :]) + _dot(fo_ref[...], w_ref[half:, :])
    h = _layer_norm(alpha * x_ref[...] + mix, g_ref[...], b_ref[...])
    h_ref[...] = h
    _store_row_tiled(hp_ref, _pack_pairs(h))


def _finish(x, ry, fo, w, g, b, alpha, tm):
    n, d = x.shape
    half = ry.shape[1]
    s = d // 2 // LANES
    row = lambda c: pl.BlockSpec((tm, c), lambda i: (i, 0))
    vec = pl.BlockSpec((1, d), lambda i: (0, 0))
    return pl.pallas_call(
        functools.partial(_finish_kernel, alpha=alpha, half=half),
        grid=(n // tm,),
        in_specs=[row(d), row(half), row(fo.shape[1]), pl.BlockSpec(w.shape, lambda i: (0, 0)), vec, vec],
        out_specs=[row(d), pl.BlockSpec((tm * s, LANES), lambda i: (i, 0))],
        out_shape=[jax.ShapeDtypeStruct((n, d), F32), jax.ShapeDtypeStruct((n * s, LANES), U32)],
        compiler_params=_cparams(("arbitrary",)),
    )(x, ry, fo, w, g, b)


def _router_kernel(h_ref, whi_ref, wlo_ref, eb_ref, c0_ref, idx_ref, gate_ref, rank_ref, cnt_ref, carry,
                   *, n_exp, tm):
    @pl.when(pl.program_id(0) == 0)
    def _():
        carry[...] = c0_ref[...].astype(F32)

    h = h_ref[...]
    hhi = h.astype(BF16)
    hlo = (h - hhi.astype(F32)).astype(BF16)
    whi = whi_ref[...]
    logits = _dot_nt(whi, hhi) + _dot_nt(whi, hlo) + _dot_nt(wlo_ref[...], hhi)
    scores = jax.nn.sigmoid(logits)
    sel = scores + eb_ref[...]
    gsz = n_exp // N_GROUPS
    eio = lax.broadcasted_iota(I32, (n_exp, tm), 0).astype(F32)
    gio = lax.broadcasted_iota(I32, (gsz, tm), 0).astype(F32)
    gs_rows = []
    for g in range(N_GROUPS):
        sg = sel[g * gsz:(g + 1) * gsz]
        m1 = jnp.max(sg, axis=0, keepdims=True)
        i1 = jnp.min(jnp.where(sg == m1, gio, float(gsz)), axis=0, keepdims=True)
        m2 = jnp.max(jnp.where(gio == i1, -jnp.inf, sg), axis=0, keepdims=True)
        gs_rows.append(m1 + m2)
    gs = jnp.concatenate(gs_rows, axis=0)
    grow = lax.broadcasted_iota(I32, (N_GROUPS, tm), 0)
    beaten = jnp.zeros((N_GROUPS, tm), F32)
    for g2 in range(N_GROUPS):
        o = gs_rows[g2]
        beats = jnp.where(o > gs, 1.0, jnp.where((o == gs) & (grow > g2), 1.0, 0.0))
        beaten = beaten + beats
    gkeep = jnp.where(beaten < float(TOPK_GROUPS), 1.0, 0.0)
    selm = jnp.concatenate(
        [jnp.where(gkeep[g:g + 1] > 0.5, sel[g * gsz:(g + 1) * gsz], -jnp.inf) for g in range(N_GROUPS)],
        axis=0)
    member = jnp.zeros((n_exp, tm), F32)
    idxs, gates = [], []
    for _ in range(TOP_K):
        m = jnp.max(selm, axis=0, keepdims=True)
        ik = jnp.min(jnp.where(selm == m, eio, float(n_exp)), axis=0, keepdims=True)
        hit = eio == ik
        gates.append(jnp.sum(jnp.where(hit, scores, 0.0), axis=0, keepdims=True))
        idxs.append(ik)
        selm = jnp.where(hit, -jnp.inf, selm)
        member = jnp.where(hit, 1.0, member)
    gsum = gates[0]
    for gk in gates[1:]:
        gsum = gsum + gk
    tr = lax.broadcasted_iota(I32, (tm, tm), 0)
    tc = lax.broadcasted_iota(I32, (tm, tm), 1)
    before = jnp.where(tr < tc, 1.0, 0.0).astype(BF16)
    prefix = _dot(member.astype(BF16), before) + carry[:, :1]
    ranks = [jnp.sum(jnp.where(eio == ik, prefix, 0.0), axis=0, keepdims=True) for ik in idxs]
    carry[...] = carry[...] + jnp.sum(member, axis=1, keepdims=True)
    idx_ref[...] = jnp.concatenate(idxs, axis=0).astype(I32)
    gate_ref[...] = jnp.concatenate([gk / gsum * ROUTED_SCALE for gk in gates], axis=0)
    rank_ref[...] = jnp.concatenate(ranks, axis=0).astype(I32)
    cnt_ref[...] = carry[...].astype(I32)


def _router(h, whi, wlo, eb, cnt0, tm):
    n, d = h.shape
    n_exp = whi.shape[0]
    tok = pl.BlockSpec((TOP_K, tm), lambda i: (0, i))
    full = lambda s: pl.BlockSpec(s, lambda i: (0, 0))
    return pl.pallas_call(
        functools.partial(_router_kernel, n_exp=n_exp, tm=tm),
        grid=(n // tm,),
        in_specs=[pl.BlockSpec((tm, d), lambda i: (i, 0)), full((n_exp, d)), full((n_exp, d)),
                  full((n_exp, 1)), full((n_exp, LANES))],
        out_specs=[tok, tok, tok, full((n_exp, LANES))],
        out_shape=[jax.ShapeDtypeStruct((TOP_K, n), I32), jax.ShapeDtypeStruct((TOP_K, n), F32),
                   jax.ShapeDtypeStruct((TOP_K, n), I32), jax.ShapeDtypeStruct((n_exp, LANES), I32)],
        scratch_shapes=[pltpu.VMEM((n_exp, LANES), F32)],
        compiler_params=_cparams(("arbitrary",)),
    )(h, whi, wlo, eb, cnt0)


def _dest_kernel(idx_ref, rank_ref, ps_ref, o_ref, *, n_exp, tm):
    eio = lax.broadcasted_iota(I32, (n_exp, tm), 0)
    ps = ps_ref[...]
    rows = []
    for k in range(TOP_K):
        hit = eio == idx_ref[k:k + 1, :]
        rows.append(jnp.sum(jnp.where(hit, ps, 0.0), axis=0, keepdims=True))
    o_ref[...] = jnp.concatenate(rows, axis=0).astype(I32) + rank_ref[...]


def _dest(idx, rank, pstart, tm):
    n = idx.shape[1]
    n_exp = pstart.shape[0]
    tok = pl.BlockSpec((TOP_K, tm), lambda i: (0, i))
    return pl.pallas_call(
        functools.partial(_dest_kernel, n_exp=n_exp, tm=tm),
        grid=(n // tm,),
        in_specs=[tok, tok, pl.BlockSpec((n_exp, 1), lambda i: (0, 0))],
        out_specs=tok,
        out_shape=jax.ShapeDtypeStruct((TOP_K, n), I32),
        compiler_params=_cparams(("arbitrary",)),
    )(idx, rank, pstart.astype(F32).reshape(n_exp, 1))


def _zero_pads_kernel(last_ref, has_ref, xs_ref, zbuf, sem):
    e = pl.program_id(0)
    n_exp = pl.num_programs(0)

    def zero_copy(start):
        return pltpu.make_async_copy(zbuf, xs_ref.at[pl.ds(pl.multiple_of(start, EXPERT_ROWS), EXPERT_ROWS)], sem)

    @pl.when(e == 0)
    def _():
        zbuf[...] = jnp.zeros_like(zbuf)

    @pl.when(has_ref[e] > 0)
    def _():
        zero_copy(last_ref[e]).start()

    @pl.when(e == n_exp - 1)
    def _():
        def drain(j, carry):
            @pl.when(has_ref[j] > 0)
            def _():
                zero_copy(0).wait()
            return carry
        lax.fori_loop(0, n_exp, drain, 0)


def _zero_pads(last_start, has_rows, n_rows, half):
    n_exp = last_start.shape[0]
    s = half // LANES
    grid_spec = pltpu.PrefetchScalarGridSpec(
        num_scalar_prefetch=2,
        grid=(n_exp,),
        in_specs=[],
        out_specs=pl.BlockSpec(memory_space=pl.ANY),
        scratch_shapes=[pltpu.VMEM((EXPERT_ROWS, s, LANES), U32), pltpu.SemaphoreType.DMA(())],
    )
    return pl.pallas_call(
        _zero_pads_kernel,
        grid_spec=grid_spec,
        out_shape=jax.ShapeDtypeStruct((n_rows, s, LANES), U32),
        compiler_params=_cparams(("arbitrary",)),
    )(last_start, has_rows)


def _dispatch_kernel(dest_ref, hp_ref, xs_in_ref, xs_ref, dsm, sem_d, sem, *, tm):
    del xs_in_ref
    i = pl.program_id(0)
    cp = pltpu.make_async_copy(dest_ref.at[i], dsm, sem_d)
    cp.start()
    cp.wait()

    def row_copy(t, d):
        return pltpu.make_async_copy(hp_ref.at[t], xs_ref.at[d], sem)

    def issue(t, carry):
        for k in range(TOP_K):
            row_copy(t, dsm[k, t]).start()
        return carry

    lax.fori_loop(0, tm, issue, 0)

    def drain(t, carry):
        for k in range(TOP_K):
            row_copy(0, 0).wait()
        return carry

    lax.fori_loop(0, tm, drain, 0)


def _dispatch(dest_tiles, hp, xs, tm):
    n, s, _ = hp.shape
    return pl.pallas_call(
        functools.partial(_dispatch_kernel, tm=tm),
        grid=(n // tm,),
        in_specs=[pl.BlockSpec(memory_space=pl.ANY),
                  pl.BlockSpec((tm, s, LANES), lambda i: (i, 0, 0)),
                  pl.BlockSpec(memory_space=pl.ANY)],
        out_specs=pl.BlockSpec(memory_space=pl.ANY),
        out_shape=jax.ShapeDtypeStruct(xs.shape, U32),
        scratch_shapes=[pltpu.SMEM((TOP_K, tm), I32), pltpu.SemaphoreType.DMA(()), pltpu.SemaphoreType.DMA(())],
        input_output_aliases={2: 0},
        compiler_params=_cparams(("arbitrary",)),
    )(dest_tiles, hp, xs)


def _experts_kernel(be_ref, nu_ref, nxt_ref, x_ref, wg_hbm, wu_hbm, wd_hbm, y_ref,
                    wg_f, wu_f, wd_f, wg_s, wu_s, wd_s, slot_ref, sem, *, half, n_exp):
    b = pl.program_id(0)
    active = b < nu_ref[0]
    e = be_ref[b]
    new_expert = (b == 0) | (e != be_ref[jnp.maximum(b - 1, 0)])

    def weight_copies(expert, s):
        return (pltpu.make_async_copy(wg_hbm.at[expert], wg_f.at[s], sem.at[s]),
                pltpu.make_async_copy(wu_hbm.at[expert], wu_f.at[s], sem.at[s]),
                pltpu.make_async_copy(wd_hbm.at[expert], wd_f.at[s], sem.at[s]))

    @pl.when(active & (b == 0))
    def _():
        slot_ref[0] = 0
        for cp in weight_copies(e, 0):
            cp.start()

    @pl.when(active & new_expert)
    def _():
        s = slot_ref[0]
        for cp in weight_copies(e, s):
            cp.wait()
        nxt = nxt_ref[e]

        @pl.when(nxt < n_exp)
        def _():
            for cp in weight_copies(nxt, 1 - s):
                cp.start()

        wg_s[...] = wg_f[s].astype(BF16)
        wu_s[...] = wu_f[s].astype(BF16)
        wd_s[...] = wd_f[s].astype(BF16)
        slot_ref[0] = 1 - s

    @pl.when(active)
    def _():
        lo, hi = _unpack_pairs(_load_row_tiled(x_ref, 0, EXPERT_ROWS, half // LANES))
        lo = lo.astype(BF16)
        hi = hi.astype(BF16)
        g = _dot(lo, wg_s[:half, :]) + _dot(hi, wg_s[half:, :])
        u = _dot(lo, wu_s[:half, :]) + _dot(hi, wu_s[half:, :])
        hm = (_silu(g) * u).astype(BF16)
        _store_row_tiled(y_ref, _pack_pairs(_dot(hm, wd_s[...])))


def _experts(block_e, n_used, next_expert, xs, wg, wu, wd):
    n_exp, d, de = wg.shape
    half = d // 2
    s = half // LANES
    n_rows = xs.shape[0] // s
    nb = n_rows // EXPERT_ROWS
    blk = lambda b, be, nu, nx: (jnp.minimum(b, nu[0] - 1), 0)
    hbm = pl.BlockSpec(memory_space=pl.ANY)
    grid_spec = pltpu.PrefetchScalarGridSpec(
        num_scalar_prefetch=3,
        grid=(nb,),
        in_specs=[pl.BlockSpec((EXPERT_ROWS * s, LANES), blk), hbm, hbm, hbm],
        out_specs=pl.BlockSpec((EXPERT_ROWS * s, LANES), blk),
        scratch_shapes=[pltpu.VMEM((2, d, de), F32), pltpu.VMEM((2, d, de), F32), pltpu.VMEM((2, de, d), F32),
                        pltpu.VMEM((d, de), BF16), pltpu.VMEM((d, de), BF16), pltpu.VMEM((de, d), BF16),
                        pltpu.SMEM((1,), I32), pltpu.SemaphoreType.DMA((2,))],
    )
    return pl.pallas_call(
        functools.partial(_experts_kernel, half=half, n_exp=n_exp),
        grid_spec=grid_spec,
        out_shape=jax.ShapeDtypeStruct((n_rows * s, LANES), U32),
        compiler_params=_cparams(("arbitrary",)),
    )(block_e, n_used, next_expert, xs, wg, wu, wd)


def _combine_kernel(dest_ref, h_ref, gate_ref, ys_ref, wsg_ref, wsu_ref, wsd_ref, g_ref, b_ref, y_ref,
                    buf, dsm, sem_d, sem, *, tm, alpha):
    i = pl.program_id(0)
    last = pl.num_programs(0) - 1
    slot = i % 2
    nslot = 1 - slot

    def table_copy(tile, s):
        return pltpu.make_async_copy(dest_ref.at[jnp.minimum(tile, last)], dsm.at[s], sem_d.at[s])

    rs = ys_ref.shape[1]

    def row_copy(s, k, t, d):
        start = (k * tm + t) * rs
        if not isinstance(start, int):
            start = pl.multiple_of(start, rs)
        return pltpu.make_async_copy(ys_ref.at[d], buf.at[s, pl.ds(start, rs)], sem.at[s])

    def wait_rows(s):
        def drain(t, carry):
            for k in range(TOP_K):
                row_copy(s, 0, 0, 0).wait()
            return carry
        lax.fori_loop(0, tm, drain, 0)

    @pl.when(i == 0)
    def _():
        table_copy(0, 0).start()
        table_copy(0, 0).wait()

        def issue(t, carry):
            for k in range(TOP_K):
                row_copy(0, k, t, dsm[0, k, t]).start()
            return carry
        lax.fori_loop(0, tm, issue, 0)
        table_copy(1, 1).start()

    table_copy(i + 1, nslot).wait()
    wait_rows(slot)
    table_copy(i + 2, slot).start()

    for t in range(tm):
        for k in range(TOP_K):
            row_copy(nslot, k, t, dsm[nslot, k, t]).start()

    h = h_ref[...]
    hb = h.astype(BF16)
    sh = _dot((_silu(_dot(hb, wsg_ref[...])) * _dot(hb, wsu_ref[...])).astype(BF16), wsd_ref[...])

    gate = gate_ref[...]
    rows = buf.at[slot]
    acc_lo = jnp.zeros((tm, rs * LANES), F32)
    acc_hi = jnp.zeros((tm, rs * LANES), F32)
    for k in range(TOP_K):
        lo, hi = _unpack_pairs(_load_row_tiled(rows, k * tm * rs, tm, rs))
        gk = gate[:, k:k + 1]
        acc_lo = acc_lo + gk * lo
        acc_hi = acc_hi + gk * hi
    routed = jnp.concatenate([acc_lo, acc_hi], axis=1)
    y_ref[...] = _layer_norm(alpha * h + (routed + sh), g_ref[...], b_ref[...])

    @pl.when(i == last)
    def _():
        wait_rows(nslot)
        table_copy(i + 2, slot).wait()


def _combine(dest_tiles, h, gate_t, ys, wsg, wsu, wsd, g, b, alpha, tm):
    n, d = h.shape
    rs = ys.shape[1]
    full = lambda a: pl.BlockSpec(a.shape, lambda i: (0, 0))
    return pl.pallas_call(
        functools.partial(_combine_kernel, tm=tm, alpha=alpha),
        grid=(n // tm,),
        in_specs=[pl.BlockSpec(memory_space=pl.ANY),
                  pl.BlockSpec((tm, d), lambda i: (i, 0)),
                  pl.BlockSpec((tm, TOP_K), lambda i: (i, 0)),
                  pl.BlockSpec(memory_space=pl.ANY),
                  full(wsg), full(wsu), full(wsd), full(g), full(b)],
        out_specs=pl.BlockSpec((tm, d), lambda i: (i, 0)),
        out_shape=jax.ShapeDtypeStruct((n, d), F32),
        scratch_shapes=[pltpu.VMEM((2, TOP_K * tm * rs, LANES), U32), pltpu.SMEM((2, TOP_K, tm), I32),
                        pltpu.SemaphoreType.DMA((2,)), pltpu.SemaphoreType.DMA((2,))],
        compiler_params=_cparams(("arbitrary",)),
    )(dest_tiles, h, gate_t, ys, wsg, wsu, wsd, g, b)


def _rope_tables(pos):
    half = HEAD_DIM // 2
    inv_freq = ROPE_BASE ** (-jnp.arange(half, dtype=F32) / half)
    ang = pos.astype(F32)[:, None] * inv_freq[None, :]
    cos, sin = jnp.cos(ang), jnp.sin(ang)
    return jnp.concatenate([cos, cos], axis=-1), jnp.concatenate([-sin, sin], axis=-1)


def _dest_tiles(dest, tm):
    k, n = dest.shape
    return dest.reshape(k, n // tm, tm).transpose(1, 0, 2)


def _layer(xp, xs, state_ret, cache_k, cache_v, cache_logf, w_in, b_fgate, ret_gn_g, w_out,
           ln1_g, ln1_b, w_router, e_bias, w_e_gate, w_e_up, w_e_down, w_s_gate, w_s_up, w_s_down,
           ln2_g, ln2_b, alpha):
    bp, tp, d = xp.shape
    bs, ts, _ = xs.shape
    past = cache_k.shape[1]
    width = w_out.shape[0] // 2
    n_heads = width // HEAD_DIM
    n_exp = w_router.shape[1]

    w_ret = w_in[:, :4 * width].astype(BF16)
    w_fox = w_in[:, 4 * width:7 * width].astype(BF16)
    n_f = w_in.shape[1] - 7 * width
    w_f = jnp.pad(w_in[:, 7 * width:], ((0, 0), (0, LANES - n_f))).astype(BF16)
    b_f = jnp.pad(b_fgate, (0, LANES - n_f)).reshape(1, LANES)
    gn = ret_gn_g.reshape(1, width)
    w_o = w_out.astype(BF16)
    l1g, l1b = ln1_g.reshape(1, d), ln1_b.reshape(1, d)
    l2g, l2b = ln2_g.reshape(1, d), ln2_b.reshape(1, d)
    wr_t = w_router.T
    wr_top = _truncate_to_bf16(wr_t)
    wr_hi = wr_top.astype(BF16)
    wr_lo = (wr_t - wr_top).astype(BF16)
    eb = e_bias.reshape(n_exp, 1)
    wsg, wsu, wsd = w_s_gate.astype(BF16), w_s_up.astype(BF16), w_s_down.astype(BF16)

    xp2 = xp.reshape(bp * tp, d)
    tm_p = _tile(tp, 512)
    cs_p, sn_p = _rope_tables(jnp.arange(tp))
    pr = _proj_ret(xp2, w_ret, cs_p, sn_p, tp, tm_p).reshape(bp, tp, 4 * width)
    fk, fv, lf, c_p, qa_p, ka_p, vt_p, nrm_p = _proj_fox_heads(xp2, w_fox, w_f, b_f, bp, _tile(tp, 256))
    s0 = jnp.zeros((bp, n_heads, HEAD_DIM, HEAD_DIM), F32)
    ry_p, sfin_p = _retention(pr, s0, gn, _tile(tp, 256))
    lf_p = lf.reshape(bp, tp, LANES)
    tq_p = _tile(tp, 2048)
    skip_p = _fox_skip_table(nrm_p, c_p.reshape(bp, tp, LANES), n_heads, tq_p, tq_p // 2)
    fo_p = _fox(qa_p, ka_p, vt_p, 0, tq_p, tq_p // 2, n_chains=2, skip=skip_p)
    h_p, hp_p = _finish(xp2, ry_p.reshape(bp * tp, width), fo_p.reshape(bp * tp, width), w_o, l1g, l1b,
                        alpha, _tile(bp * tp, 256))

    xs2 = xs.reshape(bs * ts, d)
    tm_s = _tile(ts, 512)
    cs_s, sn_s = _rope_tables(past + jnp.arange(ts))
    prs = _proj_ret(xs2, w_ret, cs_s, sn_s, ts, tm_s).reshape(bs, ts, 4 * width)
    fq_s, fk_s, fv_s, fkb_s, fvb_s, lf_sn = _proj_fox(xs2, w_fox, w_f, b_f, tm_s)
    ry_s, sfin_s = _retention(prs, state_ret.astype(F32), gn, ts)
    tk_s = LANES
    tq_s = -(-ts // LANES) * LANES
    t_all = -(-(past + tq_s) // tk_s) * tk_s
    pad_t = t_all - past - ts
    lf_s3 = lf_sn.reshape(bs, ts, LANES)
    lf_all = jnp.concatenate([
        jnp.pad(cache_logf.astype(F32), ((0, 0), (0, 0), (0, LANES - n_heads))),
        lf_s3, jnp.zeros((bs, pad_t, LANES), F32)], axis=1)
    c_s = _cumsum(lf_all, _tile(t_all, 1088))
    zpad = jnp.zeros((bs, pad_t, width), BF16)
    k_all = jnp.concatenate([cache_k.reshape(bs, past, width).astype(BF16),
                             fkb_s.reshape(bs, ts, width), zpad], axis=1)
    v_all = jnp.concatenate([cache_v.reshape(bs, past, width).astype(BF16),
                             fvb_s.reshape(bs, ts, width), zpad], axis=1)
    qa_s, ka_s, vt_s = _fox_operands(fq_s.reshape(bs, ts, width), k_all, v_all, c_s, tq_s, past)
    fo_s = _fox(qa_s, ka_s, vt_s, past, tq_s, t_all)[:, :ts]
    h_s, hp_s = _finish(xs2, ry_s.reshape(bs * ts, width), fo_s.reshape(bs * ts, width), w_o, l1g, l1b,
                        alpha, _tile(bs * ts, 256))

    n_p, n_s = bp * tp, bs * ts
    tm_r = 256
    cnt0 = jnp.zeros((n_exp, LANES), I32)
    idx_p, gate_p, rank_p, cnt1 = _router(h_p, wr_hi, wr_lo, eb, cnt0, _tile(n_p, tm_r))
    idx_s, gate_s, rank_s, cnt2 = _router(h_s, wr_hi, wr_lo, eb, cnt1, _tile(n_s, tm_r))
    counts = cnt2[:, 0]
    padded = (counts + EXPERT_ROWS - 1) // EXPERT_ROWS * EXPERT_ROWS
    pend = jnp.cumsum(padded)
    pstart = pend - padded
    dest_p = _dest(idx_p, rank_p, pstart, _tile(n_p, 512))
    dest_s = _dest(idx_s, rank_s, pstart, _tile(n_s, 512))
    n_blocks = -(-((n_p + n_s) * TOP_K) // EXPERT_ROWS) + n_exp
    n_used = (pend[-1] // EXPERT_ROWS).astype(I32).reshape(1)
    block_start = jnp.arange(n_blocks, dtype=I32) * EXPERT_ROWS
    block_e = jnp.minimum(jnp.sum((pend[None, :] <= block_start[:, None]).astype(I32), axis=1),
                          n_exp - 1).astype(I32)
    tm_d = 512
    dt_p = _dest_tiles(dest_p, _tile(n_p, tm_d))
    dt_s = _dest_tiles(dest_s, _tile(n_s, tm_d))
    rs = d // 2 // LANES
    n_rows = n_blocks * EXPERT_ROWS
    xs_rows = _zero_pads((pend - EXPERT_ROWS).astype(I32), padded.astype(I32), n_rows, d // 2)
    xs_rows = _dispatch(dt_p, hp_p.reshape(n_p, rs, LANES), xs_rows, _tile(n_p, tm_d))
    xs_rows = _dispatch(dt_s, hp_s.reshape(n_s, rs, LANES), xs_rows, _tile(n_s, tm_d))
    xs_rows = xs_rows.reshape(n_rows * rs, LANES)
    owners = jnp.where(counts > 0, jnp.arange(n_exp, dtype=I32), n_exp)
    later = lax.cummin(owners[::-1])[::-1]
    next_expert = jnp.concatenate([later[1:], jnp.full((1,), n_exp, I32)]).astype(I32)
    ys_rows = _experts(block_e, n_used, next_expert, xs_rows, w_e_gate, w_e_up, w_e_down)
    ys_rows = ys_rows.reshape(n_rows, rs, LANES)
    tm_c = 128
    y_p = _combine(_dest_tiles(dest_p, _tile(n_p, tm_c)), h_p, gate_p.T, ys_rows, wsg, wsu, wsd, l2g, l2b,
                   alpha, _tile(n_p, tm_c))
    y_s = _combine(_dest_tiles(dest_s, _tile(n_s, tm_c)), h_s, gate_s.T, ys_rows, wsg, wsu, wsd, l2g, l2b,
                   alpha, _tile(n_s, tm_c))

    outs_p = (sfin_p, fk.reshape(bp, tp, n_heads, HEAD_DIM), fv.reshape(bp, tp, n_heads, HEAD_DIM),
              lf_p[:, :, :n_heads])
    outs_s = (sfin_s, fk_s.reshape(bs, ts, n_heads, HEAD_DIM), fv_s.reshape(bs, ts, n_heads, HEAD_DIM),
              lf_s3[:, :, :n_heads])
    return y_p.reshape(bp, tp, d), y_s.reshape(bs, ts, d), outs_p, outs_s


def kernel(x_prompt, x_sample, state_ret, cache_fox_k, cache_fox_v, cache_fox_logf, w_in, b_fgate, ret_gn_g, w_out, ln1_g, ln1_b, w_router, e_bias, w_e_gate, w_e_up, w_e_down, w_s_gate, w_s_up, w_s_down, ln2_g, ln2_b):
    depth = w_in.shape[0]
    alpha = (2.0 * depth) ** 0.25
    xp, xs = x_prompt, x_sample
    per_p, per_s = [], []
    for l in range(depth):
        xp, xs, op, os_ = _layer(
            xp, xs, state_ret[l], cache_fox_k[l], cache_fox_v[l], cache_fox_logf[l], w_in[l], b_fgate[l],
            ret_gn_g[l], w_out[l], ln1_g[l], ln1_b[l], w_router[l], e_bias[l], w_e_gate[l], w_e_up[l],
            w_e_down[l], w_s_gate[l], w_s_up[l], w_s_down[l], ln2_g[l], ln2_b[l], alpha)
        per_p.append(op)
        per_s.append(os_)
    stack = lambda items, j: jnp.stack([it[j] for it in items])
    return (xp, xs,
            stack(per_p, 0), stack(per_p, 1), stack(per_p, 2), stack(per_p, 3),
            stack(per_s, 0).astype(state_ret.dtype), stack(per_s, 1), stack(per_s, 2),
            stack(per_s, 3).astype(cache_fox_logf.dtype))
```

```python
import functools
import math

import jax
import jax.numpy as jnp
from jax import lax
from jax.experimental import pallas as pl
from jax.experimental.pallas import tpu as pltpu

HEAD_DIM = 128
ROPE_BASE = 10000.0
N_GROUPS = 8
TOPK_GROUPS = 4
TOP_K = 8
ROUTED_SCALE = 2.5
LN_EPS = 1e-5
LANES = 128
EXPERT_ROWS = 256
LOG2_E = math.log2(math.e)
VMEM_LIMIT_BYTES = 56 * 1024 * 1024

F32 = jnp.float32
BF16 = jnp.bfloat16
U32 = jnp.uint32
I32 = jnp.int32


def _cparams(sem):
    return pltpu.CompilerParams(dimension_semantics=sem, vmem_limit_bytes=VMEM_LIMIT_BYTES)


def _tile(n, pref):
    t = min(n, pref)
    while n % t:
        t -= 1
    return t


def _dot(a, b):
    return jnp.dot(a, b, preferred_element_type=F32)


def _dot_nt(a, b):
    return lax.dot_general(a, b, (((1,), (1,)), ((), ())), preferred_element_type=F32)


def _dot_tn(a, b):
    return lax.dot_general(a, b, (((0,), (0,)), ((), ())), preferred_element_type=F32)


def _pack_pairs(a):
    n = a.shape[1] // 2
    lo = lax.bitcast_convert_type(a[:, :n].astype(BF16).astype(F32), U32)
    hi = lax.bitcast_convert_type(a[:, n:].astype(BF16).astype(F32), U32)
    return (lo >> 16) | (hi & jnp.uint32(0xFFFF0000))


def _unpack_pairs(u):
    lo = lax.bitcast_convert_type(u << 16, F32)
    hi = lax.bitcast_convert_type(u & jnp.uint32(0xFFFF0000), F32)
    return lo, hi


def _layer_norm(z, g, b):
    mu = jnp.mean(z, axis=-1, keepdims=True)
    zc = z - mu
    var = jnp.mean(zc * zc, axis=-1, keepdims=True)
    return zc * lax.rsqrt(var + LN_EPS) * g + b


def _silu(g):
    return g * jax.nn.sigmoid(g)


def _proj_ret_kernel(x_ref, w_ref, cs_ref, sn_ref, o_ref, *, width, scale):
    xb = x_ref[...].astype(BF16)
    cs = cs_ref[...]
    sn = sn_ref[...]
    for sec in range(4):
        p = _dot(xb, w_ref[:, sec * width:(sec + 1) * width])
        if sec < 2:
            for h in range(width // HEAD_DIM):
                ph = p[:, h * HEAD_DIM:(h + 1) * HEAD_DIM]
                r = ph * cs + pltpu.roll(ph, HEAD_DIM // 2, 1) * sn
                if sec == 1:
                    r = r * scale
                o_ref[:, sec * width + h * HEAD_DIM:sec * width + (h + 1) * HEAD_DIM] = r.astype(BF16)
        else:
            o_ref[:, sec * width:(sec + 1) * width] = p.astype(BF16)


def _proj_ret(x, w, cs, sn, seq, tm):
    n, d = x.shape
    width = w.shape[1] // 4
    nt_seq = seq // tm
    return pl.pallas_call(
        functools.partial(_proj_ret_kernel, width=width, scale=HEAD_DIM ** -0.5),
        grid=(n // tm,),
        in_specs=[
            pl.BlockSpec((tm, d), lambda i: (i, 0)),
            pl.BlockSpec((d, 4 * width), lambda i: (0, 0)),
            pl.BlockSpec((tm, HEAD_DIM), lambda i: (i % nt_seq, 0)),
            pl.BlockSpec((tm, HEAD_DIM), lambda i: (i % nt_seq, 0)),
        ],
        out_specs=pl.BlockSpec((tm, 4 * width), lambda i: (i, 0)),
        out_shape=jax.ShapeDtypeStruct((n, 4 * width), BF16),
        compiler_params=_cparams(("arbitrary",)),
    )(x, w, cs, sn)


def _proj_fox_kernel(x_ref, w_ref, wf_ref, bf_ref, q_ref, k_ref, v_ref, kb_ref, vb_ref, lf_ref,
                     *, width, scale):
    xb = x_ref[...].astype(BF16)
    q = _dot(xb, w_ref[:, :width])
    q_ref[...] = (q * scale).astype(BF16)
    k = _dot(xb, w_ref[:, width:2 * width])
    k_ref[...] = k
    kb_ref[...] = k.astype(BF16)
    v = _dot(xb, w_ref[:, 2 * width:])
    v_ref[...] = v
    vb_ref[...] = v.astype(BF16)
    z = _dot(xb, wf_ref[...]) + bf_ref[...]
    lf_ref[...] = jnp.minimum(z, 0.0) - jnp.log1p(jnp.exp(-jnp.abs(z)))


def _log_sigmoid(z):
    return jnp.minimum(z, 0.0) - jnp.log1p(jnp.exp(-jnp.abs(z)))


def _proj_fox_heads_kernel(x_ref, w_ref, wf_ref, bf_ref, k_ref, v_ref, lf_ref, c_ref, qa_ref, ka_ref, vt_ref,
                           nrm_ref, carry, *, width, scale, n_heads):
    @pl.when(pl.program_id(1) == 0)
    def _():
        carry[...] = jnp.zeros_like(carry)

    tm = x_ref.shape[0]
    xb = x_ref[...].astype(BF16)
    q = _dot(xb, w_ref[:, :width]) * scale
    k = _dot(xb, w_ref[:, width:2 * width])
    v = _dot(xb, w_ref[:, 2 * width:])
    k_ref[...] = k
    v_ref[...] = v
    logf = _log_sigmoid(_dot(xb, wf_ref[...]) + bf_ref[...])
    lf_ref[...] = logf
    c = _tile_cumsum(logf, carry[...])
    c_ref[...] = c
    carry[...] = c[tm - 1:tm, :]
    c2 = c * LOG2_E
    lane = lax.broadcasted_iota(I32, (tm, HEAD_DIM), 1)
    lane_row = lax.broadcasted_iota(I32, (1, LANES), 1)

    def max_sq_norm(a):
        af = a.astype(F32)
        return jnp.max(jnp.sum(af * af, axis=-1, keepdims=True), axis=0, keepdims=True)

    qn = jnp.zeros((1, LANES), F32)
    kn = jnp.zeros((1, LANES), F32)
    for h in range(n_heads):
        sl = slice(h * HEAD_DIM, (h + 1) * HEAD_DIM)
        hi, mid, lo = _split3(c2[:, h:h + 1])
        terms = jnp.where(lane == 0, hi.astype(F32), jnp.where(lane == 1, mid.astype(F32), lo.astype(F32)))
        q_tail = jnp.where(lane < 3, terms, jnp.where(lane < 6, 1.0, 0.0))
        k_tail = jnp.where(lane < 3, 1.0, jnp.where(lane < 6, -pltpu.roll(terms, 3, 1), 0.0))
        qh = q[:, sl].astype(BF16)
        kh = k[:, sl].astype(BF16)
        qa_ref[0, h, :, :HEAD_DIM] = qh
        qa_ref[0, h, :, HEAD_DIM:] = q_tail.astype(BF16)
        ka_ref[0, h, :, :HEAD_DIM] = kh
        ka_ref[0, h, :, HEAD_DIM:] = k_tail.astype(BF16)
        vt_ref[0, h] = v[:, sl].T.astype(BF16)
        qn = jnp.where(lane_row == h, max_sq_norm(qh), qn)
        kn = jnp.where(lane_row == h, max_sq_norm(kh), kn)

    @pl.when(pl.program_id(1) == 0)
    def _():
        nrm_ref[...] = jnp.zeros_like(nrm_ref)

    nrm_ref[0, 0:1, :] = jnp.maximum(nrm_ref[0, 0:1, :], qn)
    nrm_ref[0, 1:2, :] = jnp.maximum(nrm_ref[0, 1:2, :], kn)


def _proj_fox_heads(x, w, wf, bfg, batch, tm):
    n, d = x.shape
    width = w.shape[1] // 3
    n_heads = width // HEAD_DIM
    seq = n // batch
    nt = seq // tm
    row = lambda c: pl.BlockSpec((tm, c), lambda b, i: (b * nt + i, 0))
    const = lambda s: pl.BlockSpec(s, lambda b, i: (0, 0))
    aug = pl.BlockSpec((1, n_heads, tm, 2 * HEAD_DIM), lambda b, i: (b, 0, i, 0))
    return pl.pallas_call(
        functools.partial(_proj_fox_heads_kernel, width=width, scale=HEAD_DIM ** -0.5 * LOG2_E,
                          n_heads=n_heads),
        grid=(batch, nt),
        in_specs=[row(d), const((d, 3 * width)), const((d, LANES)), const((1, LANES))],
        out_specs=[row(width), row(width), row(LANES), row(LANES), aug, aug,
                   pl.BlockSpec((1, n_heads, HEAD_DIM, tm), lambda b, i: (b, 0, 0, i)),
                   pl.BlockSpec((1, 8, LANES), lambda b, i: (b, 0, 0))],
        out_shape=[
            jax.ShapeDtypeStruct((n, width), F32),
            jax.ShapeDtypeStruct((n, width), F32),
            jax.ShapeDtypeStruct((n, LANES), F32),
            jax.ShapeDtypeStruct((n, LANES), F32),
            jax.ShapeDtypeStruct((batch, n_heads, seq, 2 * HEAD_DIM), BF16),
            jax.ShapeDtypeStruct((batch, n_heads, seq, 2 * HEAD_DIM), BF16),
            jax.ShapeDtypeStruct((batch, n_heads, HEAD_DIM, seq), BF16),
            jax.ShapeDtypeStruct((batch, 8, LANES), F32),
        ],
        scratch_shapes=[pltpu.VMEM((1, LANES), F32)],
        compiler_params=_cparams(("arbitrary", "arbitrary")),
    )(x, w, wf, bfg)


def _proj_fox(x, w, wf, bfg, tm):
    n, d = x.shape
    width = w.shape[1] // 3
    row = lambda c: pl.BlockSpec((tm, c), lambda i: (i, 0))
    return pl.pallas_call(
        functools.partial(_proj_fox_kernel, width=width, scale=HEAD_DIM ** -0.5 * LOG2_E),
        grid=(n // tm,),
        in_specs=[
            row(d),
            pl.BlockSpec((d, 3 * width), lambda i: (0, 0)),
            pl.BlockSpec((d, LANES), lambda i: (0, 0)),
            pl.BlockSpec((1, LANES), lambda i: (0, 0)),
        ],
        out_specs=[row(width), row(width), row(width), row(width), row(width), row(LANES)],
        out_shape=[
            jax.ShapeDtypeStruct((n, width), BF16),
            jax.ShapeDtypeStruct((n, width), F32),
            jax.ShapeDtypeStruct((n, width), F32),
            jax.ShapeDtypeStruct((n, width), BF16),
            jax.ShapeDtypeStruct((n, width), BF16),
            jax.ShapeDtypeStruct((n, LANES), F32),
        ],
        compiler_params=_cparams(("arbitrary",)),
    )(x, w, wf, bfg)


def _split3(x):
    hi = x.astype(BF16)
    r1 = x - hi.astype(F32)
    mid = r1.astype(BF16)
    lo = (r1 - mid.astype(F32)).astype(BF16)
    return hi, mid, lo


def _tile_cumsum(x, carry_row):
    tm = x.shape[0]
    r = lax.broadcasted_iota(I32, (tm, tm), 0)
    c = lax.broadcasted_iota(I32, (tm, tm), 1)
    tri = jnp.where(c <= r, 1.0, 0.0).astype(BF16)
    hi, mid, lo = _split3(x)
    return _dot(tri, hi) + _dot(tri, mid) + _dot(tri, lo) + carry_row


def _cumsum_kernel(x_ref, o_ref, carry, *, tm):
    @pl.when(pl.program_id(1) == 0)
    def _():
        carry[...] = jnp.zeros_like(carry)

    out = _tile_cumsum(x_ref[0], carry[...])
    o_ref[0] = out
    carry[...] = out[tm - 1:tm, :]


def _cumsum(x, tm):
    b, t, _ = x.shape
    return pl.pallas_call(
        functools.partial(_cumsum_kernel, tm=tm),
        grid=(b, t // tm),
        in_specs=[pl.BlockSpec((1, tm, LANES), lambda i, j: (i, j, 0))],
        out_specs=pl.BlockSpec((1, tm, LANES), lambda i, j: (i, j, 0)),
        out_shape=jax.ShapeDtypeStruct(x.shape, F32),
        scratch_shapes=[pltpu.VMEM((1, LANES), F32)],
        compiler_params=_cparams(("arbitrary", "arbitrary")),
    )(x)


def _retention_kernel(q_ref, k_ref, v_ref, g_ref, s0_ref, gn_ref, y_ref, sout_ref, s_scr, decay_scr,
                      *, n_heads, chunk):
    c = pl.program_id(1)

    @pl.when(c == 0)
    def _():
        s_scr[...] = s0_ref[0]

    @pl.when((pl.program_id(0) == 0) & (c == 0))
    def _():
        row = lax.broadcasted_iota(I32, (chunk, chunk), 0)
        col = lax.broadcasted_iota(I32, (chunk, chunk), 1)
        rel = (row - col).astype(F32)
        for h in range(n_heads):
            lg = math.log1p(-(2.0 ** (-5 - h)))
            decay_scr[h] = jnp.where(rel >= 0, jnp.exp(lg * jnp.maximum(rel, 0.0)), 0.0)

    ri = lax.broadcasted_iota(I32, (chunk, HEAD_DIM), 0).astype(F32)
    for h in range(n_heads):
        sl = slice(h * HEAD_DIM, (h + 1) * HEAD_DIM)
        lg = math.log1p(-(2.0 ** (-5 - h)))
        decay = decay_scr[h]
        q = q_ref[0, :, sl]
        k = k_ref[0, :, sl]
        v = v_ref[0, :, sl]
        state = s_scr[h]
        scores = _dot_nt(q, k) * decay
        o = _dot(scores.astype(BF16), v)
        o = o + jnp.exp(lg * (ri + 1.0)) * _dot(q, state.astype(BF16))
        kd = (k.astype(F32) * jnp.exp(lg * (chunk - 1.0 - ri))).astype(BF16)
        s_scr[h] = math.exp(lg * chunk) * state + _dot_tn(kd, v)
        mu = jnp.mean(o, axis=-1, keepdims=True)
        oc = o - mu
        var = jnp.mean(oc * oc, axis=-1, keepdims=True)
        yn = oc * lax.rsqrt(var + LN_EPS) * gn_ref[:, sl]
        y_ref[0, :, sl] = (_silu(g_ref[0, :, sl].astype(F32)) * yn).astype(BF16)

    @pl.when(c == pl.num_programs(1) - 1)
    def _():
        sout_ref[0] = s_scr[...]


def _retention(p, s0, gn, chunk):
    b, t, w4 = p.shape
    width = w4 // 4
    n_heads = width // HEAD_DIM
    sec = lambda s: pl.BlockSpec((1, chunk, width), lambda i, j: (i, j, s))
    st = pl.BlockSpec((1, n_heads, HEAD_DIM, HEAD_DIM), lambda i, j: (i, 0, 0, 0))
    return pl.pallas_call(
        functools.partial(_retention_kernel, n_heads=n_heads, chunk=chunk),
        grid=(b, t // chunk),
        in_specs=[sec(0), sec(1), sec(2), sec(3), st, pl.BlockSpec((1, width), lambda i, j: (0, 0))],
        out_specs=[pl.BlockSpec((1, chunk, width), lambda i, j: (i, j, 0)), st],
        out_shape=[jax.ShapeDtypeStruct((b, t, width), BF16),
                   jax.ShapeDtypeStruct(s0.shape, F32)],
        scratch_shapes=[pltpu.VMEM((n_heads, HEAD_DIM, HEAD_DIM), F32), pltpu.VMEM((n_heads, chunk, chunk), F32)],
        compiler_params=_cparams(("arbitrary", "arbitrary")),
    )(p, p, p, p, s0, gn)


def _fox_kernel(skip_ref, q_ref, k_ref, vt_ref, o_ref, *, tq, tk, q_off, n_kblocks, n_chains):
    qi = pl.program_id(2)
    tqc = tq // n_chains

    def absorb(s, q_first, vt, k0, carry, masked):
        m, l, acc = carry
        if masked:
            kpos = k0 + lax.broadcasted_iota(I32, (tk, tqc), 0)
            qpos = q_first + lax.broadcasted_iota(I32, (tk, tqc), 1)
            s = jnp.where(kpos <= qpos, s, -jnp.inf)
        m_new = jnp.maximum(m, jnp.max(s, axis=0, keepdims=True))
        alpha = jnp.exp2(m - m_new)
        p = jnp.exp2(s - m_new)
        l = alpha * l + jnp.sum(p, axis=0, keepdims=True)
        acc = alpha * acc + _dot(vt, p.astype(BF16))
        return m_new, l, acc

    def kv_block(j):
        k0 = pl.multiple_of(j * tk, tk)
        return k_ref[0, 0, pl.ds(k0, tk), :], vt_ref[0, 0, :, pl.ds(k0, tk)], k0

    init =(jnp.full((1, tqc), -jnp.inf, F32), jnp.zeros((1, tqc), F32), jnp.zeros((HEAD_DIM, tqc), F32))

    if n_chains == 1:
        q = q_ref[0, 0]
        q_first = q_off + qi * tq
        n_full = jnp.minimum((q_first + 1) // tk, n_kblocks)
        n_tot = jnp.minimum((q_first + tq + tk - 1) // tk, n_kblocks)

        def step(j, carry, masked):
            k, vt, k0 = kv_block(j)
            return absorb(_dot_nt(k, q), q_first, vt, k0, carry, masked)

        carry = lax.fori_loop(0, n_full, functools.partial(step, masked=False), init)
        _, l, acc = lax.fori_loop(n_full, n_tot, functools.partial(step, masked=True), carry)
        o_ref[0] = (acc / l).T.astype(BF16)
    else:
        qs = [q_ref[0, 0, c * tqc:(c + 1) * tqc, :] for c in range(n_chains)]
        firsts = [qi * tq + c * tqc for c in range(n_chains)]

        def step(j, carries):
            k, vt, k0 = kv_block(j)
            scores = [_dot_nt(k, qs[c]) for c in range(n_chains)]
            return tuple(absorb(scores[c], firsts[c], vt, k0, carries[c], False) for c in range(n_chains))

        first = skip_ref[(pl.program_id(0) * pl.num_programs(1) + pl.program_id(1)) * pl.num_programs(2) + qi]
        carries = list(lax.fori_loop(first, qi * n_chains, step, (init,) * n_chains))
        for jj in range(n_chains):
            k, vt, k0 = kv_block(qi * n_chains + jj)
            scores = {c: _dot_nt(k, qs[c]) for c in range(jj, n_chains)}
            for c in range(jj, n_chains):
                carries[c] = absorb(scores[c], firsts[c], vt, k0, carries[c], c == jj)
        for c in range(n_chains):
            _, l, acc = carries[c]
            o_ref[0, c * tqc:(c + 1) * tqc, :] = (acc / l).T.astype(BF16)


def _fox(qa, ka, vt, q_off, tq, tk, n_chains=1, skip=None):
    b, n_heads, t_q, da = qa.shape
    t_k = ka.shape[2]
    assert n_chains == 1 or (q_off == 0 and tq == n_chains * tk and t_q == t_k)
    if skip is None:
        skip = jnp.zeros((b * n_heads * (t_q // tq),), I32)
    grid_spec = pltpu.PrefetchScalarGridSpec(
        num_scalar_prefetch=1,
        grid=(b, n_heads, t_q // tq),
        in_specs=[
            pl.BlockSpec((1, 1, tq, da), lambda i, h, j, sk: (i, h, j, 0)),
            pl.BlockSpec((1, 1, t_k, da), lambda i, h, j, sk: (i, h, 0, 0)),
            pl.BlockSpec((1, 1, HEAD_DIM, t_k), lambda i, h, j, sk: (i, h, 0, 0)),
        ],
        out_specs=pl.BlockSpec((1, tq, HEAD_DIM), lambda i, h, j, sk: (i, j, h)),
    )
    return pl.pallas_call(
        functools.partial(_fox_kernel, tq=tq, tk=tk, q_off=q_off, n_kblocks=t_k // tk, n_chains=n_chains),
        grid_spec=grid_spec,
        out_shape=jax.ShapeDtypeStruct((b, t_q, n_heads * HEAD_DIM), BF16),
        compiler_params=_cparams(("arbitrary", "arbitrary", "arbitrary")),
    )(skip, qa, ka, vt)


UNDERFLOW_LOG2 = 160.0


def _fox_skip_table(sq_norms, c, n_heads, tq, tk):
    t = c.shape[1]
    bound = jnp.sqrt(sq_norms[:, 0, :n_heads] * sq_norms[:, 1, :n_heads])
    c2 = (c[:, :, :n_heads] * LOG2_E).transpose(0, 2, 1)
    c_end = c2[:, :, tk - 1::tk]
    c_q0 = c2[:, :, ::tq]
    gap = 2.0 * bound[:, :, None, None] + c_q0[:, :, :, None] - c_end[:, :, None, :]
    before = (jnp.arange(t // tk)[None, :] < (jnp.arange(t // tq) * (tq // tk))[:, None])
    dead = (gap < -UNDERFLOW_LOG2) & before[None, None]
    return jnp.sum(jnp.cumprod(dead.astype(I32), axis=-1), axis=-1).astype(I32).reshape(-1)


def _truncate_to_bf16(x):
    bits = lax.bitcast_convert_type(x, U32) & jnp.uint32(0xFFFF0000)
    return lax.bitcast_convert_type(bits, F32)


def _fox_operands(fq, fkb, fvb, c, q_rows, q_off):
    b, t_k, width = fkb.shape
    n_heads = width // HEAD_DIM
    t_q = fq.shape[1]
    c2 = (c[:, :, :n_heads] * LOG2_E).transpose(0, 2, 1)
    hi = _truncate_to_bf16(c2)
    r1 = c2 - hi
    mid = _truncate_to_bf16(r1)
    lo = r1 - mid
    lane = jnp.arange(HEAD_DIM)

    def tail(first, sign, rows):
        pick = lambda a: a[:, :, rows, None]
        terms = jnp.where(lane == first, pick(hi), jnp.where(lane == first + 1, pick(mid), pick(lo)))
        is_term = (lane >= first) & (lane < first + 3)
        return jnp.where(is_term, sign * terms, jnp.where(lane < 6, 1.0, 0.0)).astype(BF16)

    heads = lambda a: a.reshape(b, a.shape[1], n_heads, HEAD_DIM).transpose(0, 2, 1, 3)
    ka = jnp.concatenate([heads(fkb), tail(3, -1.0, slice(None))], axis=-1)
    qh = jnp.pad(heads(fq), ((0, 0), (0, 0), (0, q_rows - t_q), (0, 0)))
    qa = jnp.concatenate([qh, tail(0, 1.0, slice(q_off, q_off + q_rows))], axis=-1)
    vt = fvb.reshape(b, t_k, n_heads, HEAD_DIM).transpose(0, 2, 3, 1)
    return qa, ka, vt


def _store_row_tiled(ref, a):
    m, w = a.shape
    s = w // LANES
    for j in range(s):
        ref[pl.ds(j, m, stride=s), :] = a[:, j * LANES:(j + 1) * LANES]


def _load_row_tiled(ref, start, m, s):
    return jnp.concatenate([ref[pl.ds(start + j, m, stride=s), :] for j in range(s)], axis=1)


def _finish_kernel(x_ref, ry_ref, fo_ref, w_ref, g_ref, b_ref, h_ref, hp_ref, *, alpha, half):
    mix = _dot(ry_ref[...], w_ref[:half, :]) + _dot(fo_ref[...], w_ref[half:, :])
    h = _layer_norm(alpha * x_ref[...] + mix, g_ref[...], b_ref[...])
    h_ref[...] = h
    _store_row_tiled(hp_ref, _pack_pairs(h))


def _finish(x, ry, fo, w, g, b, alpha, tm):
    n, d = x.shape
    half = ry.shape[1]
    s = d // 2 // LANES
    row = lambda c: pl.BlockSpec((tm, c), lambda i: (i, 0))
    vec = pl.BlockSpec((1, d), lambda i: (0, 0))
    return pl.pallas_call(
        functools.partial(_finish_kernel, alpha=alpha, half=half),
        grid=(n // tm,),
        in_specs=[row(d), row(half), row(fo.shape[1]), pl.BlockSpec(w.shape, lambda i: (0, 0)), vec, vec],
        out_specs=[row(d), pl.BlockSpec((tm * s, LANES), lambda i: (i, 0))],
        out_shape=[jax.ShapeDtypeStruct((n, d), F32), jax.ShapeDtypeStruct((n * s, LANES), U32)],
        compiler_params=_cparams(("arbitrary",)),
    )(x, ry, fo, w, g, b)


def _router_kernel(h_ref, whi_ref, wlo_ref, eb_ref, c0_ref, idx_ref, gate_ref, rank_ref, cnt_ref, carry,
                   *, n_exp, tm):
    @pl.when(pl.program_id(0) == 0)
    def _():
        carry[...] = c0_ref[...].astype(F32)

    h = h_ref[...]
    hhi = h.astype(BF16)
    hlo = (h - hhi.astype(F32)).astype(BF16)
    whi = whi_ref[...]
    logits = _dot_nt(whi, hhi) + _dot_nt(whi, hlo) + _dot_nt(wlo_ref[...], hhi)
    scores = jax.nn.sigmoid(logits)
    sel = scores + eb_ref[...]
    gsz = n_exp // N_GROUPS
    eio = lax.broadcasted_iota(I32, (n_exp, tm), 0).astype(F32)
    gio = lax.broadcasted_iota(I32, (gsz, tm), 0).astype(F32)
    gs_rows = []
    for g in range(N_GROUPS):
        sg = sel[g * gsz:(g + 1) * gsz]
        m1 = jnp.max(sg, axis=0, keepdims=True)
        i1 = jnp.min(jnp.where(sg == m1, gio, float(gsz)), axis=0, keepdims=True)
        m2 = jnp.max(jnp.where(gio == i1, -jnp.inf, sg), axis=0, keepdims=True)
        gs_rows.append(m1 + m2)
    gs = jnp.concatenate(gs_rows, axis=0)
    grow = lax.broadcasted_iota(I32, (N_GROUPS, tm), 0)
    beaten = jnp.zeros((N_GROUPS, tm), F32)
    for g2 in range(N_GROUPS):
        o = gs_rows[g2]
        beats = jnp.where(o > gs, 1.0, jnp.where((o == gs) & (grow > g2), 1.0, 0.0))
        beaten = beaten + beats
    gkeep = jnp.where(beaten < float(TOPK_GROUPS), 1.0, 0.0)
    selm = jnp.concatenate(
        [jnp.where(gkeep[g:g + 1] > 0.5, sel[g * gsz:(g + 1) * gsz], -jnp.inf) for g in range(N_GROUPS)],
        axis=0)
    member = jnp.zeros((n_exp, tm), F32)
    idxs, gates = [], []
    for _ in range(TOP_K):
        m = jnp.max(selm, axis=0, keepdims=True)
        ik = jnp.min(jnp.where(selm == m, eio, float(n_exp)), axis=0, keepdims=True)
        hit = eio == ik
        gates.append(jnp.sum(jnp.where(hit, scores, 0.0), axis=0, keepdims=True))
        idxs.append(ik)
        selm = jnp.where(hit, -jnp.inf, selm)
        member = jnp.where(hit, 1.0, member)
    gsum = gates[0]
    for gk in gates[1:]:
        gsum = gsum + gk
    tr = lax.broadcasted_iota(I32, (tm, tm), 0)
    tc = lax.broadcasted_iota(I32, (tm, tm), 1)
    before = jnp.where(tr < tc, 1.0, 0.0).astype(BF16)
    prefix = _dot(member.astype(BF16), before) + carry[:, :1]
    ranks = [jnp.sum(jnp.where(eio == ik, prefix, 0.0), axis=0, keepdims=True) for ik in idxs]
    carry[...] = carry[...] + jnp.sum(member, axis=1, keepdims=True)
    idx_ref[...] = jnp.concatenate(idxs, axis=0).astype(I32)
    gate_ref[...] = jnp.concatenate([gk / gsum * ROUTED_SCALE for gk in gates], axis=0)
    rank_ref[...] = jnp.concatenate(ranks, axis=0).astype(I32)
    cnt_ref[...] = carry[...].astype(I32)


def _router(h, whi, wlo, eb, cnt0, tm):
    n, d = h.shape
    n_exp = whi.shape[0]
    tok = pl.BlockSpec((TOP_K, tm), lambda i: (0, i))
    full = lambda s: pl.BlockSpec(s, lambda i: (0, 0))
    return pl.pallas_call(
        functools.partial(_router_kernel, n_exp=n_exp, tm=tm),
        grid=(n // tm,),
        in_specs=[pl.BlockSpec((tm, d), lambda i: (i, 0)), full((n_exp, d)), full((n_exp, d)),
                  full((n_exp, 1)), full((n_exp, LANES))],
        out_specs=[tok, tok, tok, full((n_exp, LANES))],
        out_shape=[jax.ShapeDtypeStruct((TOP_K, n), I32), jax.ShapeDtypeStruct((TOP_K, n), F32),
                   jax.ShapeDtypeStruct((TOP_K, n), I32), jax.ShapeDtypeStruct((n_exp, LANES), I32)],
        scratch_shapes=[pltpu.VMEM((n_exp, LANES), F32)],
        compiler_params=_cparams(("arbitrary",)),
    )(h, whi, wlo, eb, cnt0)


def _dest_kernel(idx_ref, rank_ref, ps_ref, o_ref, *, n_exp, tm):
    eio = lax.broadcasted_iota(I32, (n_exp, tm), 0)
    ps = ps_ref[...]
    rows = []
    for k in range(TOP_K):
        hit = eio == idx_ref[k:k + 1, :]
        rows.append(jnp.sum(jnp.where(hit, ps, 0.0), axis=0, keepdims=True))
    o_ref[...] = jnp.concatenate(rows, axis=0).astype(I32) + rank_ref[...]


def _dest(idx, rank, pstart, tm):
    n = idx.shape[1]
    n_exp = pstart.shape[0]
    tok = pl.BlockSpec((TOP_K, tm), lambda i: (0, i))
    return pl.pallas_call(
        functools.partial(_dest_kernel, n_exp=n_exp, tm=tm),
        grid=(n // tm,),
        in_specs=[tok, tok, pl.BlockSpec((n_exp, 1), lambda i: (0, 0))],
        out_specs=tok,
        out_shape=jax.ShapeDtypeStruct((TOP_K, n), I32),
        compiler_params=_cparams(("arbitrary",)),
    )(idx, rank, pstart.astype(F32).reshape(n_exp, 1))


def _zero_pads_kernel(last_ref, has_ref, xs_ref, zbuf, sem):
    e = pl.program_id(0)
    n_exp = pl.num_programs(0)

    def zero_copy(start):
        return pltpu.make_async_copy(zbuf, xs_ref.at[pl.ds(pl.multiple_of(start, EXPERT_ROWS), EXPERT_ROWS)], sem)

    @pl.when(e == 0)
    def _():
        zbuf[...] = jnp.zeros_like(zbuf)

    @pl.when(has_ref[e] > 0)
    def _():
        zero_copy(last_ref[e]).start()

    @pl.when(e == n_exp - 1)
    def _():
        def drain(j, carry):
            @pl.when(has_ref[j] > 0)
            def _():
                zero_copy(0).wait()
            return carry
        lax.fori_loop(0, n_exp, drain, 0)


def _zero_pads(last_start, has_rows, n_rows, half):
    n_exp = last_start.shape[0]
    s = half // LANES
    grid_spec = pltpu.PrefetchScalarGridSpec(
        num_scalar_prefetch=2,
        grid=(n_exp,),
        in_specs=[],
        out_specs=pl.BlockSpec(memory_space=pl.ANY),
        scratch_shapes=[pltpu.VMEM((EXPERT_ROWS, s, LANES), U32), pltpu.SemaphoreType.DMA(())],
    )
    return pl.pallas_call(
        _zero_pads_kernel,
        grid_spec=grid_spec,
        out_shape=jax.ShapeDtypeStruct((n_rows, s, LANES), U32),
        compiler_params=_cparams(("arbitrary",)),
    )(last_start, has_rows)


def _dispatch_kernel(dest_ref, hp_ref, xs_in_ref, xs_ref, dsm, sem_d, sem, *, tm):
    del xs_in_ref
    i = pl.program_id(0)
    cp = pltpu.make_async_copy(dest_ref.at[i], dsm, sem_d)
    cp.start()
    cp.wait()

    def row_copy(t, d):
        return pltpu.make_async_copy(hp_ref.at[t], xs_ref.at[d], sem)

    def issue(t, carry):
        for k in range(TOP_K):
            row_copy(t, dsm[k, t]).start(priority=k % 2)
        return carry

    lax.fori_loop(0, tm, issue, 0)

    def drain(t, carry):
        for k in range(TOP_K):
            row_copy(0, 0).wait()
        return carry

    lax.fori_loop(0, tm, drain, 0)


def _dispatch(dest_tiles, hp, xs, tm):
    n, s, _ = hp.shape
    return pl.pallas_call(
        functools.partial(_dispatch_kernel, tm=tm),
        grid=(n // tm,),
        in_specs=[pl.BlockSpec(memory_space=pl.ANY),
                  pl.BlockSpec((tm, s, LANES), lambda i: (i, 0, 0)),
                  pl.BlockSpec(memory_space=pl.ANY)],
        out_specs=pl.BlockSpec(memory_space=pl.ANY),
        out_shape=jax.ShapeDtypeStruct(xs.shape, U32),
        scratch_shapes=[pltpu.SMEM((TOP_K, tm), I32), pltpu.SemaphoreType.DMA(()), pltpu.SemaphoreType.DMA(())],
        input_output_aliases={2: 0},
        compiler_params=_cparams(("arbitrary",)),
    )(dest_tiles, hp, xs)


def _experts_kernel(be_ref, nu_ref, nxt_ref, x_ref, wg_hbm, wu_hbm, wd_hbm, y_ref,
                    wg_f, wu_f, wd_f, wg_s, wu_s, wd_s, slot_ref, sem, *, half, n_exp):
    b = pl.program_id(0)
    active = b < nu_ref[0]
    e = be_ref[b]
    new_expert = (b == 0) | (e != be_ref[jnp.maximum(b - 1, 0)])

    def weight_copies(expert, s):
        return (pltpu.make_async_copy(wg_hbm.at[expert], wg_f.at[s], sem.at[s]),
                pltpu.make_async_copy(wu_hbm.at[expert], wu_f.at[s], sem.at[s]),
                pltpu.make_async_copy(wd_hbm.at[expert], wd_f.at[s], sem.at[s]))

    @pl.when(active & (b == 0))
    def _():
        slot_ref[0] = 0
        for cp in weight_copies(e, 0):
            cp.start()

    @pl.when(active & new_expert)
    def _():
        s = slot_ref[0]
        for cp in weight_copies(e, s):
            cp.wait()
        nxt = nxt_ref[e]

        @pl.when(nxt < n_exp)
        def _():
            for cp in weight_copies(nxt, 1 - s):
                cp.start()

        wg_s[...] = wg_f[s].astype(BF16)
        wu_s[...] = wu_f[s].astype(BF16)
        wd_s[...] = wd_f[s].astype(BF16)
        slot_ref[0] = 1 - s

    @pl.when(active)
    def _():
        lo, hi = _unpack_pairs(_load_row_tiled(x_ref, 0, EXPERT_ROWS, half // LANES))
        lo = lo.astype(BF16)
        hi = hi.astype(BF16)
        g = _dot(lo, wg_s[:half, :]) + _dot(hi, wg_s[half:, :])
        u = _dot(lo, wu_s[:half, :]) + _dot(hi, wu_s[half:, :])
        hm = (_silu(g) * u).astype(BF16)
        _store_row_tiled(y_ref, _pack_pairs(_dot(hm, wd_s[...])))


def _experts(block_e, n_used, next_expert, xs, wg, wu, wd):
    n_exp, d, de = wg.shape
    half = d // 2
    s = half // LANES
    n_rows = xs.shape[0] // s
    nb = n_rows // EXPERT_ROWS
    blk = lambda b, be, nu, nx: (jnp.minimum(b, nu[0] - 1), 0)
    hbm = pl.BlockSpec(memory_space=pl.ANY)
    grid_spec = pltpu.PrefetchScalarGridSpec(
        num_scalar_prefetch=3,
        grid=(nb,),
        in_specs=[pl.BlockSpec((EXPERT_ROWS * s, LANES), blk), hbm, hbm, hbm],
        out_specs=pl.BlockSpec((EXPERT_ROWS * s, LANES), blk),
        scratch_shapes=[pltpu.VMEM((2, d, de), F32), pltpu.VMEM((2, d, de), F32), pltpu.VMEM((2, de, d), F32),
                        pltpu.VMEM((d, de), BF16), pltpu.VMEM((d, de), BF16), pltpu.VMEM((de, d), BF16),
                        pltpu.SMEM((1,), I32), pltpu.SemaphoreType.DMA((2,))],
    )
    return pl.pallas_call(
        functools.partial(_experts_kernel, half=half, n_exp=n_exp),
        grid_spec=grid_spec,
        out_shape=jax.ShapeDtypeStruct((n_rows * s, LANES), U32),
        compiler_params=_cparams(("arbitrary",)),
    )(block_e, n_used, next_expert, xs, wg, wu, wd)


def _combine_kernel(dest_ref, h_ref, gate_ref, ys_ref, wsg_ref, wsu_ref, wsd_ref, g_ref, b_ref, y_ref,
                    buf, dsm, sem_d, sem, *, tm, alpha):
    i = pl.program_id(0)
    last = pl.num_programs(0) - 1
    slot = i % 2
    nslot = 1 - slot

    def table_copy(tile, s):
        return pltpu.make_async_copy(dest_ref.at[jnp.minimum(tile, last)], dsm.at[s], sem_d.at[s])

    rs = ys_ref.shape[1]

    def row_copy(s, k, t, d):
        start = (k * tm + t) * rs
        if not isinstance(start, int):
            start = pl.multiple_of(start, rs)
        return pltpu.make_async_copy(ys_ref.at[d], buf.at[s, pl.ds(start, rs)], sem.at[s])

    def wait_rows(s):
        def drain(t, carry):
            for k in range(TOP_K):
                row_copy(s, 0, 0, 0).wait()
            return carry
        lax.fori_loop(0, tm, drain, 0)

    @pl.when(i == 0)
    def _():
        table_copy(0, 0).start()
        table_copy(0, 0).wait()

        def issue(t, carry):
            for k in range(TOP_K):
                row_copy(0, k, t, dsm[0, k, t]).start(priority=k % 2)
            return carry
        lax.fori_loop(0, tm, issue, 0)
        table_copy(1, 1).start()

    table_copy(i + 1, nslot).wait()
    wait_rows(slot)
    table_copy(i + 2, slot).start()

    for t in range(tm):
        for k in range(TOP_K):
            row_copy(nslot, k, t, dsm[nslot, k, t]).start(priority=k % 2)

    h = h_ref[...]
    hb = h.astype(BF16)
    sh = _dot((_silu(_dot(hb, wsg_ref[...])) * _dot(hb, wsu_ref[...])).astype(BF16), wsd_ref[...])

    gate = gate_ref[...]
    rows = buf.at[slot]
    acc_lo = jnp.zeros((tm, rs * LANES), F32)
    acc_hi = jnp.zeros((tm, rs * LANES), F32)
    for k in range(TOP_K):
        lo, hi = _unpack_pairs(_load_row_tiled(rows, k * tm * rs, tm, rs))
        gk = gate[:, k:k + 1]
        acc_lo = acc_lo + gk * lo
        acc_hi = acc_hi + gk * hi
    routed = jnp.concatenate([acc_lo, acc_hi], axis=1)
    y_ref[...] = _layer_norm(alpha * h + (routed + sh), g_ref[...], b_ref[...])

    @pl.when(i == last)
    def _():
        wait_rows(nslot)
        table_copy(i + 2, slot).wait()


def _combine(dest_tiles, h, gate_t, ys, wsg, wsu, wsd, g, b, alpha, tm):
    n, d = h.shape
    rs = ys.shape[1]
    full = lambda a: pl.BlockSpec(a.shape, lambda i: (0, 0))
    return pl.pallas_call(
        functools.partial(_combine_kernel, tm=tm, alpha=alpha),
        grid=(n // tm,),
        in_specs=[pl.BlockSpec(memory_space=pl.ANY),
                  pl.BlockSpec((tm, d), lambda i: (i, 0)),
                  pl.BlockSpec((tm, TOP_K), lambda i: (i, 0)),
                  pl.BlockSpec(memory_space=pl.ANY),
                  full(wsg), full(wsu), full(wsd), full(g), full(b)],
        out_specs=pl.BlockSpec((tm, d), lambda i: (i, 0)),
        out_shape=jax.ShapeDtypeStruct((n, d), F32),
        scratch_shapes=[pltpu.VMEM((2, TOP_K * tm * rs, LANES), U32), pltpu.SMEM((2, TOP_K, tm), I32),
                        pltpu.SemaphoreType.DMA((2,)), pltpu.SemaphoreType.DMA((2,))],
        compiler_params=_cparams(("arbitrary",)),
    )(dest_tiles, h, gate_t, ys, wsg, wsu, wsd, g, b)


def _rope_tables(pos):
    half = HEAD_DIM // 2
    inv_freq = ROPE_BASE ** (-jnp.arange(half, dtype=F32) / half)
    ang = pos.astype(F32)[:, None] * inv_freq[None, :]
    cos, sin = jnp.cos(ang), jnp.sin(ang)
    return jnp.concatenate([cos, cos], axis=-1), jnp.concatenate([-sin, sin], axis=-1)


def _dest_tiles(dest, tm):
    k, n = dest.shape
    return dest.reshape(k, n // tm, tm).transpose(1, 0, 2)


def _layer(xp, xs, state_ret, cache_k, cache_v, cache_logf, w_in, b_fgate, ret_gn_g, w_out,
           ln1_g, ln1_b, w_router, e_bias, w_e_gate, w_e_up, w_e_down, w_s_gate, w_s_up, w_s_down,
           ln2_g, ln2_b, alpha):
    bp, tp, d = xp.shape
    bs, ts, _ = xs.shape
    past = cache_k.shape[1]
    width = w_out.shape[0] // 2
    n_heads = width // HEAD_DIM
    n_exp = w_router.shape[1]

    w_ret = w_in[:, :4 * width].astype(BF16)
    w_fox = w_in[:, 4 * width:7 * width].astype(BF16)
    n_f = w_in.shape[1] - 7 * width
    w_f = jnp.pad(w_in[:, 7 * width:], ((0, 0), (0, LANES - n_f))).astype(BF16)
    b_f = jnp.pad(b_fgate, (0, LANES - n_f)).reshape(1, LANES)
    gn = ret_gn_g.reshape(1, width)
    w_o = w_out.astype(BF16)
    l1g, l1b = ln1_g.reshape(1, d), ln1_b.reshape(1, d)
    l2g, l2b = ln2_g.reshape(1, d), ln2_b.reshape(1, d)
    wr_t = w_router.T
    wr_top = _truncate_to_bf16(wr_t)
    wr_hi = wr_top.astype(BF16)
    wr_lo = (wr_t - wr_top).astype(BF16)
    eb = e_bias.reshape(n_exp, 1)
    wsg, wsu, wsd = w_s_gate.astype(BF16), w_s_up.astype(BF16), w_s_down.astype(BF16)

    xp2 = xp.reshape(bp * tp, d)
    tm_p = _tile(tp, 512)
    cs_p, sn_p = _rope_tables(jnp.arange(tp))
    pr = _proj_ret(xp2, w_ret, cs_p, sn_p, tp, tm_p).reshape(bp, tp, 4 * width)
    fk, fv, lf, c_p, qa_p, ka_p, vt_p, nrm_p = _proj_fox_heads(xp2, w_fox, w_f, b_f, bp, _tile(tp, 256))
    s0 = jnp.zeros((bp, n_heads, HEAD_DIM, HEAD_DIM), F32)
    ry_p, sfin_p = _retention(pr, s0, gn, _tile(tp, 256))
    lf_p = lf.reshape(bp, tp, LANES)
    tq_p = _tile(tp, 2048)
    skip_p = _fox_skip_table(nrm_p, c_p.reshape(bp, tp, LANES), n_heads, tq_p, tq_p // 2)
    fo_p = _fox(qa_p, ka_p, vt_p, 0, tq_p, tq_p // 2, n_chains=2, skip=skip_p)
    h_p, hp_p = _finish(xp2, ry_p.reshape(bp * tp, width), fo_p.reshape(bp * tp, width), w_o, l1g, l1b,
                        alpha, _tile(bp * tp, 256))

    xs2 = xs.reshape(bs * ts, d)
    tm_s = _tile(ts, 512)
    cs_s, sn_s = _rope_tables(past + jnp.arange(ts))
    prs = _proj_ret(xs2, w_ret, cs_s, sn_s, ts, tm_s).reshape(bs, ts, 4 * width)
    fq_s, fk_s, fv_s, fkb_s, fvb_s, lf_sn = _proj_fox(xs2, w_fox, w_f, b_f, tm_s)
    ry_s, sfin_s = _retention(prs, state_ret.astype(F32), gn, ts)
    tk_s = LANES
    tq_s = -(-ts // LANES) * LANES
    t_all = -(-(past + tq_s) // tk_s) * tk_s
    pad_t = t_all - past - ts
    lf_s3 = lf_sn.reshape(bs, ts, LANES)
    lf_all = jnp.concatenate([
        jnp.pad(cache_logf.astype(F32), ((0, 0), (0, 0), (0, LANES - n_heads))),
        lf_s3, jnp.zeros((bs, pad_t, LANES), F32)], axis=1)
    c_s = _cumsum(lf_all, _tile(t_all, 1088))
    zpad = jnp.zeros((bs, pad_t, width), BF16)
    k_all = jnp.concatenate([cache_k.reshape(bs, past, width).astype(BF16),
                             fkb_s.reshape(bs, ts, width), zpad], axis=1)
    v_all = jnp.concatenate([cache_v.reshape(bs, past, width).astype(BF16),
                             fvb_s.reshape(bs, ts, width), zpad], axis=1)
    qa_s, ka_s, vt_s = _fox_operands(fq_s.reshape(bs, ts, width), k_all, v_all, c_s, tq_s, past)
    fo_s = _fox(qa_s, ka_s, vt_s, past, tq_s, t_all)[:, :ts]
    h_s, hp_s = _finish(xs2, ry_s.reshape(bs * ts, width), fo_s.reshape(bs * ts, width), w_o, l1g, l1b,
                        alpha, _tile(bs * ts, 256))

    n_p, n_s = bp * tp, bs * ts
    tm_r = 256
    cnt0 = jnp.zeros((n_exp, LANES), I32)
    idx_p, gate_p, rank_p, cnt1 = _router(h_p, wr_hi, wr_lo, eb, cnt0, _tile(n_p, tm_r))
    idx_s, gate_s, rank_s, cnt2 = _router(h_s, wr_hi, wr_lo, eb, cnt1, _tile(n_s, tm_r))
    counts = cnt2[:, 0]
    padded = (counts + EXPERT_ROWS - 1) // EXPERT_ROWS * EXPERT_ROWS
    pend = jnp.cumsum(padded)
    pstart = pend - padded
    dest_p = _dest(idx_p, rank_p, pstart, _tile(n_p, 512))
    dest_s = _dest(idx_s, rank_s, pstart, _tile(n_s, 512))
    n_blocks = -(-((n_p + n_s) * TOP_K) // EXPERT_ROWS) + n_exp
    n_used = (pend[-1] // EXPERT_ROWS).astype(I32).reshape(1)
    block_start = jnp.arange(n_blocks, dtype=I32) * EXPERT_ROWS
    block_e = jnp.minimum(jnp.sum((pend[None, :] <= block_start[:, None]).astype(I32), axis=1),
                          n_exp - 1).astype(I32)
    tm_d = 512
    dt_p = _dest_tiles(dest_p, _tile(n_p, tm_d))
    dt_s = _dest_tiles(dest_s, _tile(n_s, tm_d))
    rs = d // 2 // LANES
    n_rows = n_blocks * EXPERT_ROWS
    xs_rows = _zero_pads((pend - EXPERT_ROWS).astype(I32), padded.astype(I32), n_rows, d // 2)
    xs_rows = _dispatch(dt_p, hp_p.reshape(n_p, rs, LANES), xs_rows, _tile(n_p, tm_d))
    xs_rows = _dispatch(dt_s, hp_s.reshape(n_s, rs, LANES), xs_rows, _tile(n_s, tm_d))
    xs_rows = xs_rows.reshape(n_rows * rs, LANES)
    owners = jnp.where(counts > 0, jnp.arange(n_exp, dtype=I32), n_exp)
    later = lax.cummin(owners[::-1])[::-1]
    next_expert = jnp.concatenate([later[1:], jnp.full((1,), n_exp, I32)]).astype(I32)
    ys_rows = _experts(block_e, n_used, next_expert, xs_rows, w_e_gate, w_e_up, w_e_down)
    ys_rows = ys_rows.reshape(n_rows, rs, LANES)
    tm_c = 128
    y_p = _combine(_dest_tiles(dest_p, _tile(n_p, tm_c)), h_p, gate_p.T, ys_rows, wsg, wsu, wsd, l2g, l2b,
                   alpha, _tile(n_p, tm_c))
    y_s = _combine(_dest_tiles(dest_s, _tile(n_s, tm_c)), h_s, gate_s.T, ys_rows, wsg, wsu, wsd, l2g, l2b,
                   alpha, _tile(n_s, tm_c))

    outs_p = (sfin_p, fk.reshape(bp, tp, n_heads, HEAD_DIM), fv.reshape(bp, tp, n_heads, HEAD_DIM),
              lf_p[:, :, :n_heads])
    outs_s = (sfin_s, fk_s.reshape(bs, ts, n_heads, HEAD_DIM), fv_s.reshape(bs, ts, n_heads, HEAD_DIM),
              lf_s3[:, :, :n_heads])
    return y_p.reshape(bp, tp, d), y_s.reshape(bs, ts, d), outs_p, outs_s


def kernel(x_prompt, x_sample, state_ret, cache_fox_k, cache_fox_v, cache_fox_logf, w_in, b_fgate, ret_gn_g, w_out, ln1_g, ln1_b, w_router, e_bias, w_e_gate, w_e_up, w_e_down, w_s_gate, w_s_up, w_s_down, ln2_g, ln2_b):
    depth = w_in.shape[0]
    alpha = (2.0 * depth) ** 0.25
    xp, xs = x_prompt, x_sample
    per_p, per_s = [], []
    for l in range(depth):
        xp, xs, op, os_ = _layer(
            xp, xs, state_ret[l], cache_fox_k[l], cache_fox_v[l], cache_fox_logf[l], w_in[l], b_fgate[l],
            ret_gn_g[l], w_out[l], ln1_g[l], ln1_b[l], w_router[l], e_bias[l], w_e_gate[l], w_e_up[l],
            w_e_down[l], w_s_gate[l], w_s_up[l], w_s_down[l], ln2_g[l], ln2_b[l], alpha)
        per_p.append(op)
        per_s.append(os_)
    stack = lambda items, j: jnp.stack([it[j] for it in items])
    return (xp, xs,
            stack(per_p, 0), stack(per_p, 1), stack(per_p, 2), stack(per_p, 3),
            stack(per_s, 0).astype(state_ret.dtype), stack(per_s, 1), stack(per_s, 2),
            stack(per_s, 3).astype(cache_fox_logf.dtype))
```
